```python
import math
import jax, jax.numpy as jnp
from jax import lax
import numpy as np

D_MODEL = 4096
BATCH = 2
SEQ = 4096
DEPTH = 2
DEC_BATCH = 16
DEC_SEQ = 16
PAST_LEN = 1024

CHUNK = 64
N_META = 16
N_A = DEPTH // 2
N_B = DEPTH - N_A
CONV_W = 31
HEAD_DIM = 128
N_HEADS = D_MODEL // HEAD_DIM
D_FF = -(-8 * D_MODEL // (3 * 256)) * 256
Q_BLOCK = 128
ALPHA = (2.0 * DEPTH) ** 0.25
BETA = (8.0 * DEPTH) ** -0.25
LN_EPS = 1e-5

kernel_name = "yoco_conformer_conv_stick_breaking_step"


def _ln(x, g, b):
    xf = x.astype(jnp.float32)
    mu = jnp.mean(xf, -1, keepdims=True)
    var = jnp.mean(jnp.square(xf - mu), -1, keepdims=True)
    y = (xf - mu) * lax.rsqrt(var + LN_EPS)
    return (y * g.astype(jnp.float32) + b.astype(jnp.float32)).astype(x.dtype)


def _ffn_block(x, l, ffn_p):
    ln_ffn_g, ln_ffn_b, w_gate, w_up, w_down = ffn_p
    h = jax.nn.silu(x @ w_gate[l]) * (x @ w_up[l])
    return _ln(ALPHA * x + h @ w_down[l], ln_ffn_g[l], ln_ffn_b[l])


def _conv_layer(x, ctx, l, a_p, ln_mix_g, ln_mix_b, ffn_p):
    a_w_pw1, a_b_pw1, a_w_dw, a_b_dw, a_ln_g, a_ln_b, a_w_pw2, a_b_pw2 = a_p
    u = x @ a_w_pw1[l] + a_b_pw1[l]
    g = u[..., :D_MODEL] * jax.nn.sigmoid(u[..., D_MODEL:])
    gp = jnp.concatenate([ctx.astype(g.dtype), g], axis=1)
    new_ctx = gp[:, -(CONV_W - 1):]
    c = lax.conv_general_dilated(
        gp, a_w_dw[l][:, None, :].astype(gp.dtype), window_strides=(1,), padding='VALID',
        dimension_numbers=('NWC', 'WIO', 'NWC'), feature_group_count=D_MODEL)
    c = jax.nn.silu(_ln(c + a_b_dw[l], a_ln_g[l], a_ln_b[l]))
    m = c @ a_w_pw2[l] + a_b_pw2[l]
    x = _ln(ALPHA * x + m, ln_mix_g[l], ln_mix_b[l])
    return _ffn_block(x, l, ffn_p), new_ctx


def _stick_breaking(q, k, v, q_pos, k_pos):
    bsz, tq = q.shape[0], q.shape[1]
    qb = min(Q_BLOCK, tq)
    nb = -(-tq // qb)
    pad = nb * qb - tq
    q = jnp.pad(q, ((0, 0), (0, pad), (0, 0), (0, 0)))
    q_pos = jnp.concatenate([q_pos, jnp.full((pad,), -1, jnp.int32)])
    q_blocks = q.reshape(bsz, nb, qb, N_HEADS, HEAD_DIM).transpose(1, 0, 2, 3, 4)
    p_blocks = q_pos.reshape(nb, qb)
    scale = HEAD_DIM ** -0.5

    def one_block(args):
        qblk, pblk = args
        z = jnp.einsum('bqhd,bkhd->bhqk', qblk, k).astype(jnp.float32) * scale
        mask = k_pos[None, :] < pblk[:, None]
        log_beta = jax.nn.log_sigmoid(z)
        log_keep = jnp.where(mask, jax.nn.log_sigmoid(-z), 0.0)
        rc = lax.cumsum(log_keep, axis=3, reverse=True)
        excl = jnp.concatenate([rc[..., 1:], jnp.zeros_like(rc[..., :1])], axis=-1)
        a = jnp.where(mask, jnp.exp(log_beta + excl), 0.0)
        return jnp.einsum('bhqk,bkhd->bqhd', a.astype(v.dtype), v)

    out = lax.map(one_block, (q_blocks, p_blocks))
    out = out.transpose(1, 0, 2, 3, 4).reshape(bsz, nb * qb, N_HEADS, HEAD_DIM)
    return out[:, :tq]


def _kv(h, w_kv):
    bsz, t, _ = h.shape
    kv = (h @ w_kv).reshape(bsz, t, 2, N_HEADS, HEAD_DIM)
    return kv[:, :, 0], kv[:, :, 1]


def _sb_layer(x, k, v, q_pos, k_pos, j, w_q, w_o, ln_mix_g, ln_mix_b, ffn_p):
    l = N_A + j
    bsz, t, _ = x.shape
    q = (x @ w_q[j]).reshape(bsz, t, N_HEADS, HEAD_DIM)
    o = _stick_breaking(q, k, v, q_pos, k_pos).reshape(bsz, t, N_HEADS * HEAD_DIM)
    x = _ln(ALPHA * x + o @ w_o[j], ln_mix_g[l], ln_mix_b[l])
    return _ffn_block(x, l, ffn_p)


def setup_inputs(seed: int = 0) -> dict:
    key = jax.random.key(seed)
    ks = jax.random.split(key, 32)
    f32 = jnp.float32
    nrm = lambda k, s, sc: jax.random.normal(k, s, f32) * sc
    d = D_MODEL
    hw = N_HEADS * HEAD_DIM
    w_k = nrm(ks[10], (d, hw), d ** -0.5)
    w_v = nrm(ks[11], (d, hw), BETA * d ** -0.5)
    w_kv = jnp.concatenate([w_k.reshape(d, 1, hw), w_v.reshape(d, 1, hw)], axis=1).reshape(d, 2 * hw)
    return {
        "x_prompt": nrm(ks[0], (BATCH, SEQ, d), 1.0),
        "x_sample": nrm(ks[1], (DEC_BATCH, DEC_SEQ, d), 1.0),
        "state_conv": nrm(ks[2], (N_A, DEC_BATCH, CONV_W - 1, d), 0.5),
        "cache_k": nrm(ks[3], (DEC_BATCH, PAST_LEN, N_HEADS, HEAD_DIM), 1.0),
        "cache_v": nrm(ks[4], (DEC_BATCH, PAST_LEN, N_HEADS, HEAD_DIM), 0.5),
        "meta": nrm(ks[5], (N_META, d), 1.0),
        "a_w_pw1": nrm(ks[6], (N_A, d, 2 * d), d ** -0.5),
        "a_b_pw1": nrm(ks[7], (N_A, 2 * d), 0.02),
        "a_w_dw": nrm(ks[8], (N_A, CONV_W, d), CONV_W ** -0.5),
        "a_b_dw": nrm(ks[9], (N_A, d), 0.02),
        "a_ln_g": 1.0 + nrm(ks[12], (N_A, d), 0.02),
        "a_ln_b": nrm(ks[13], (N_A, d), 0.02),
        "a_w_pw2": nrm(ks[14], (N_A, d, d), BETA * d ** -0.5),
        "a_b_pw2": nrm(ks[15], (N_A, d), 0.02),
        "w_kv": w_kv,
        "w_q": nrm(ks[16], (N_B, d, hw), d ** -0.5),
        "w_o": nrm(ks[17], (N_B, hw, d), BETA * hw ** -0.5),
        "ln_mix_g": 1.0 + nrm(ks[18], (DEPTH, d), 0.02),
        "ln_mix_b": nrm(ks[19], (DEPTH, d), 0.02),
        "ln_ffn_g": 1.0 + nrm(ks[20], (DEPTH, d), 0.02),
        "ln_ffn_b": nrm(ks[21], (DEPTH, d), 0.02),
        "w_gate": nrm(ks[22], (DEPTH, d, D_FF), d ** -0.5),
        "w_up": nrm(ks[23], (DEPTH, d, D_FF), d ** -0.5),
        "w_down": nrm(ks[24], (DEPTH, D_FF, d), BETA * D_FF ** -0.5),
    }


def reference(x_prompt, x_sample, state_conv, cache_k, cache_v, meta,
              a_w_pw1, a_b_pw1, a_w_dw, a_b_dw, a_ln_g, a_ln_b, a_w_pw2, a_b_pw2,
              w_kv, w_q, w_o, ln_mix_g, ln_mix_b, ln_ffn_g, ln_ffn_b,
              w_gate, w_up, w_down):
    a_p = (a_w_pw1, a_b_pw1, a_w_dw, a_b_dw, a_ln_g, a_ln_b, a_w_pw2, a_b_pw2)
    ffn_p = (ln_ffn_g, ln_ffn_b, w_gate, w_up, w_down)

    bsz = x_prompt.shape[0]
    x = jnp.concatenate(
        [jnp.broadcast_to(meta.astype(x_prompt.dtype)[None], (bsz, N_META, D_MODEL)), x_prompt], axis=1)
    L = x.shape[1]
    ctx_list = []
    for l in range(N_A):
        x, ctx = _conv_layer(x, jnp.zeros((bsz, CONV_W - 1, D_MODEL), x.dtype), l,
                             a_p, ln_mix_g, ln_mix_b, ffn_p)
        ctx_list.append(ctx)
    k_p, v_p = _kv(x, w_kv)
    pos = jnp.arange(L, dtype=jnp.int32)
    for j in range(N_B):
        x = _sb_layer(x, k_p, v_p, pos, pos, j, w_q, w_o, ln_mix_g, ln_mix_b, ffn_p)
    y_prompt = x[:, N_META:]
    state_conv_prompt = jnp.stack(ctx_list, axis=0)

    m = meta.astype(x_sample.dtype)[None]
    for l in range(N_A):
        m, _ = _conv_layer(m, jnp.zeros((1, CONV_W - 1, D_MODEL), m.dtype), l,
                           a_p, ln_mix_g, ln_mix_b, ffn_p)
    k_m, v_m = _kv(m, w_kv)

    dbsz, t_new = x_sample.shape[0], x_sample.shape[1]
    past = cache_k.shape[1]
    xs = x_sample
    sctx_list = []
    for l in range(N_A):
        xs, ctx = _conv_layer(xs, state_conv[l], l, a_p, ln_mix_g, ln_mix_b, ffn_p)
        sctx_list.append(ctx)
    k_s, v_s = _kv(xs, w_kv)
    k_all = jnp.concatenate([jnp.broadcast_to(k_m, (dbsz, N_META, N_HEADS, HEAD_DIM)),
                             cache_k.astype(k_s.dtype), k_s], axis=1)
    v_all = jnp.concatenate([jnp.broadcast_to(v_m, (dbsz, N_META, N_HEADS, HEAD_DIM)),
                             cache_v.astype(v_s.dtype), v_s], axis=1)
    k_pos = jnp.arange(N_META + past + t_new, dtype=jnp.int32)
    q_pos = N_META + past + jnp.arange(t_new, dtype=jnp.int32)
    for j in range(N_B):
        xs = _sb_layer(xs, k_all, v_all, q_pos, k_pos, j, w_q, w_o, ln_mix_g, ln_mix_b, ffn_p)
    y_sample = xs
    state_conv_sample = jnp.stack(sctx_list, axis=0)

    return (y_prompt, y_sample, state_conv_prompt, state_conv_sample, k_p, v_p, k_s, v_s)
```

```python
import functools

import jax
import jax.numpy as jnp
from jax import lax
from jax.experimental import pallas as pl
from jax.experimental.pallas import tpu as pltpu

D_MODEL = 4096
BATCH = 2
SEQ = 4096
DEC_BATCH = 16
DEC_SEQ = 16
PAST_LEN = 1024
N_META = 16
CONV_W = 31
HEAD_DIM = 128
N_HEADS = D_MODEL // HEAD_DIM
D_FF = 11008
DEPTH = 2
ALPHA = (2.0 * DEPTH) ** 0.25
LN_EPS = 1e-5
SCALE = HEAD_DIM ** -0.5

N_PROMPT = BATCH * SEQ
N_SAMPLE = DEC_BATCH * DEC_SEQ
ROW_TILE = 512
N_ROWS = 8704
SAMPLE_ROW0 = N_PROMPT
META_ROW0 = N_PROMPT + N_SAMPLE
D_FF_PAD = 11264
CONV_HALO = 32
CONV_LEAD = CONV_HALO - (CONV_W - 1)
ATT_BLOCK = 256
SMALL_BLOCK = 128
VMEM_LIMIT = 56 * 1024 * 1024

F32 = jnp.float32
BF16 = jnp.bfloat16


def _params(n_axes):
    return pltpu.CompilerParams(
        dimension_semantics=("arbitrary",) * n_axes, vmem_limit_bytes=VMEM_LIMIT)


def _mm_kernel(x_ref, w_ref, *o_refs):
    acc = jnp.dot(x_ref[...], w_ref[...], preferred_element_type=F32)
    for o_ref in o_refs:
        o_ref[...] = acc.astype(o_ref.dtype)


def _matmul(x, w, out_dtypes, bn, name):
    m, k = x.shape
    n = w.shape[1]
    bm = ROW_TILE
    outs = pl.pallas_call(
        _mm_kernel,
        out_shape=[jax.ShapeDtypeStruct((m, n), dt) for dt in out_dtypes],
        grid=(n // bn, m // bm),
        in_specs=[pl.BlockSpec((bm, k), lambda j, i: (i, 0)),
                  pl.BlockSpec((k, bn), lambda j, i: (0, j))],
        out_specs=[pl.BlockSpec((bm, bn), lambda j, i: (i, j)) for _ in out_dtypes],
        compiler_params=_params(2),
        name=name,
    )(x, w)
    return outs


def _glu_kernel(x_ref, wa_ref, wb_ref, ba_ref, bb_ref, o_ref):
    x = x_ref[...]
    a = jnp.dot(x, wa_ref[...], preferred_element_type=F32) + ba_ref[...]
    b = jnp.dot(x, wb_ref[...], preferred_element_type=F32) + bb_ref[...]
    o_ref[...] = (a * jax.nn.sigmoid(b)).astype(o_ref.dtype)


def _swiglu_kernel(x_ref, wg_ref, wu_ref, o_ref):
    x = x_ref[...]
    g = jnp.dot(x, wg_ref[...], preferred_element_type=F32)
    u = jnp.dot(x, wu_ref[...], preferred_element_type=F32)
    o_ref[...] = (g * jax.nn.sigmoid(g) * u).astype(o_ref.dtype)


def _conv_glu(x, w1, b1):
    m, k = x.shape
    n = w1.shape[1] // 2
    bm, bn = ROW_TILE, 1024
    nb = n // bn
    return pl.pallas_call(
        _glu_kernel,
        out_shape=jax.ShapeDtypeStruct((m, n), F32),
        grid=(nb, m // bm),
        in_specs=[pl.BlockSpec((bm, k), lambda j, i: (i, 0)),
                  pl.BlockSpec((k, bn), lambda j, i: (0, j)),
                  pl.BlockSpec((k, bn), lambda j, i: (0, j + nb)),
                  pl.BlockSpec((1, bn), lambda j, i: (0, j)),
                  pl.BlockSpec((1, bn), lambda j, i: (0, j + nb))],
        out_specs=pl.BlockSpec((bm, bn), lambda j, i: (i, j)),
        compiler_params=_params(2),
        name="conv_pw1_glu",
    )(x, w1, w1, b1, b1)


def _swiglu(x, wg, wu):
    m, k = x.shape
    n = wg.shape[1]
    bm, bn = ROW_TILE, 1024
    return pl.pallas_call(
        _swiglu_kernel,
        out_shape=jax.ShapeDtypeStruct((m, n), BF16),
        grid=(n // bn, m // bm),
        in_specs=[pl.BlockSpec((bm, k), lambda j, i: (i, 0)),
                  pl.BlockSpec((k, bn), lambda j, i: (0, j)),
                  pl.BlockSpec((k, bn), lambda j, i: (0, j))],
        out_specs=pl.BlockSpec((bm, bn), lambda j, i: (i, j)),
        compiler_params=_params(2),
        name="ffn_gate_up",
    )(x, wg, wu)


LN_ROWS = 16


def _layer_norm_rows(x, g, b):
    mu = jnp.mean(x, axis=-1, keepdims=True)
    xc = x - mu
    var = jnp.mean(xc * xc, axis=-1, keepdims=True)
    return xc * lax.rsqrt(var + LN_EPS) * g + b


def _add_ln_kernel(r_ref, m_ref, bias_ref, g_ref, b_ref, *o_refs):
    n_chunks = r_ref.shape[0] // LN_ROWS

    def body(c, carry):
        rows = pl.ds(pl.multiple_of(c * LN_ROWS, LN_ROWS), LN_ROWS)
        x = ALPHA * r_ref[rows, :] + (m_ref[rows, :] + bias_ref[...])
        y = _layer_norm_rows(x, g_ref[...], b_ref[...])
        for o_ref in o_refs:
            o_ref[rows, :] = y.astype(o_ref.dtype)
        return carry

    lax.fori_loop(0, n_chunks, body, 0)


def _add_ln(r, m, bias, g, b, out_dtypes, name):
    rows, d = r.shape
    tm = 256
    row_spec = pl.BlockSpec((tm, d), lambda i: (i, 0))
    vec_spec = pl.BlockSpec((1, d), lambda i: (0, 0))
    return pl.pallas_call(
        _add_ln_kernel,
        out_shape=[jax.ShapeDtypeStruct((rows, d), dt) for dt in out_dtypes],
        grid=(rows // tm,),
        in_specs=[row_spec, row_spec, vec_spec, vec_spec, vec_spec],
        out_specs=[row_spec for _ in out_dtypes],
        compiler_params=_params(1),
        name=name,
    )(r, m, bias, g, b)


CONV_ROWS = 64
CONV_COLS = 512


def _conv_ln_silu(win_ref, wdw_ref, bdw_ref, g_ref, b_ref, o_ref, c_ref, t_rows):
    d = win_ref.shape[1]
    rh = min(CONV_ROWS, t_rows)

    def col_body(cc, carry):
        cols = pl.ds(pl.multiple_of(cc * CONV_COLS, CONV_COLS), CONV_COLS)
        for r0 in range(0, t_rows, rh):
            acc = jnp.zeros((rh, CONV_COLS), F32) + bdw_ref[:, cols]
            for k in range(CONV_W):
                r = r0 + CONV_LEAD + k
                acc = acc + wdw_ref[k:k + 1, cols] * win_ref[r:r + rh, cols]
            c_ref[r0:r0 + rh, cols] = acc
        return carry

    lax.fori_loop(0, d // CONV_COLS, col_body, 0)

    def ln_body(c, carry):
        rows = pl.ds(pl.multiple_of(c * LN_ROWS, LN_ROWS), LN_ROWS)
        y = _layer_norm_rows(c_ref[rows, :], g_ref[...], b_ref[...])
        o_ref[rows, :] = (y * jax.nn.sigmoid(y)).astype(o_ref.dtype)
        return carry

    lax.fori_loop(0, t_rows // LN_ROWS, ln_body, 0)


def _conv_prompt_kernel(prev_ref, cur_ref, first_ref, wdw_ref, bdw_ref, g_ref, b_ref,
                        o_ref, win_ref, c_ref):
    t_rows = cur_ref.shape[0]
    i = pl.program_id(1)

    @pl.when(i == 0)
    def _():
        win_ref[0:CONV_HALO, :] = first_ref[...]

    @pl.when(i > 0)
    def _():
        win_ref[0:CONV_HALO, :] = prev_ref[...]

    win_ref[CONV_HALO:CONV_HALO + t_rows, :] = cur_ref[...]
    _conv_ln_silu(win_ref, wdw_ref, bdw_ref, g_ref, b_ref, o_ref, c_ref, t_rows)


def _conv_stream_kernel(win_ref, wdw_ref, bdw_ref, g_ref, b_ref, o_ref, c_ref):
    _conv_ln_silu(win_ref, wdw_ref, bdw_ref, g_ref, b_ref, o_ref, c_ref, o_ref.shape[0])


def _conv_prompt(g, first, wdw, bdw, ln_g, ln_b):
    d = g.shape[1]
    t = 256
    tiles = SEQ // t
    halo_per_tile = t // CONV_HALO
    vec = lambda r: pl.BlockSpec((r, d), lambda b, i: (0, 0))
    return pl.pallas_call(
        _conv_prompt_kernel,
        out_shape=jax.ShapeDtypeStruct((N_PROMPT, d), BF16),
        grid=(BATCH, tiles),
        in_specs=[
            pl.BlockSpec((CONV_HALO, d),
                         lambda b, i: (jnp.maximum((b * tiles + i) * halo_per_tile - 1, 0), 0)),
            pl.BlockSpec((t, d), lambda b, i: (b * tiles + i, 0)),
            vec(CONV_HALO), vec(CONV_W), vec(1), vec(1), vec(1)],
        out_specs=pl.BlockSpec((t, d), lambda b, i: (b * tiles + i, 0)),
        scratch_shapes=[pltpu.VMEM((CONV_HALO + t, d), F32), pltpu.VMEM((t, d), F32)],
        compiler_params=_params(2),
        name="conv_prompt",
    )(g, g, first, wdw, bdw, ln_g, ln_b)


def _conv_streams(win, wdw, bdw, ln_g, ln_b):
    s, rows, d = win.shape
    t = rows - CONV_HALO
    vec = lambda r: pl.BlockSpec((r, d), lambda i: (0, 0))
    return pl.pallas_call(
        _conv_stream_kernel,
        out_shape=jax.ShapeDtypeStruct((s * t, d), BF16),
        grid=(s,),
        in_specs=[pl.BlockSpec((None, rows, d), lambda i: (i, 0, 0)),
                  vec(CONV_W), vec(1), vec(1), vec(1)],
        out_specs=pl.BlockSpec((t, d), lambda i: (i, 0)),
        scratch_shapes=[pltpu.VMEM((t, d), F32)],
        compiler_params=_params(1),
        name="conv_streams",
    )(win, wdw, bdw, ln_g, ln_b)


def _upper(n):
    r = lax.broadcasted_iota(jnp.int32, (n, n), 0)
    c = lax.broadcasted_iota(jnp.int32, (n, n), 1)
    return jnp.where(r > c, 1.0, 0.0).astype(BF16)


def _strict_lower_mask(nq, nk):
    r = lax.broadcasted_iota(jnp.int32, (nq, nk), 0)
    c = lax.broadcasted_iota(jnp.int32, (nq, nk), 1)
    return c < r


def _first_keys_mask(nq, nk, n_valid):
    return lax.broadcasted_iota(jnp.int32, (nq, nk), 1) < n_valid


def _sb_block(q, k, v, carry, upper, mask):
    z = lax.dot_general(q, k, (((1,), (1,)), ((), ())), preferred_element_type=F32) * SCALE
    t = jnp.log1p(jnp.exp(-jnp.abs(z)))
    log_beta = jnp.minimum(z, 0.0) - t
    log_keep = jnp.minimum(-z, 0.0) - t
    if mask is not None:
        log_keep = jnp.where(mask, log_keep, 0.0)
    hi = log_keep.astype(BF16)
    lo = (log_keep - hi.astype(F32)).astype(BF16)
    after = (jnp.dot(hi, upper, preferred_element_type=F32)
             + jnp.dot(lo, upper, preferred_element_type=F32))
    a = jnp.exp(log_beta + after + carry)
    if mask is not None:
        a = jnp.where(mask, a, 0.0)
    out = jnp.dot(a.astype(BF16), v, preferred_element_type=F32)
    return out, carry + jnp.sum(log_keep, axis=1, keepdims=True)


def _attn_prompt_kernel(q_ref, k_ref, v_ref, km_ref, vm_ref, o_ref):
    qi = pl.program_id(2)
    blk = ATT_BLOCK
    q = q_ref[...]
    upper = _upper(blk)
    diag = pl.ds(pl.multiple_of(qi * blk, blk), blk)
    acc, carry = _sb_block(q, k_ref[diag, :], v_ref[diag, :], jnp.zeros((blk, 1), F32),
                           upper, _strict_lower_mask(blk, blk))

    def body(step, state):
        acc, carry = state
        rows = pl.ds(pl.multiple_of((qi - 1 - step) * blk, blk), blk)
        out, carry = _sb_block(q, k_ref[rows, :], v_ref[rows, :], carry, upper, None)
        return acc + out, carry

    acc, carry = lax.fori_loop(0, qi, body, (acc, carry))
    out, _ = _sb_block(q, km_ref[...], vm_ref[...], carry, _upper(SMALL_BLOCK),
                       _first_keys_mask(blk, SMALL_BLOCK, N_META))
    o_ref[...] = (acc + out).astype(o_ref.dtype)


def _attn_prompt(q, kv):
    blk = ATT_BLOCK
    tiles = SEQ // blk
    meta_blk = META_ROW0 // SMALL_BLOCK
    return pl.pallas_call(
        _attn_prompt_kernel,
        out_shape=jax.ShapeDtypeStruct((N_PROMPT, D_MODEL), BF16),
        grid=(BATCH, N_HEADS, tiles),
        in_specs=[
            pl.BlockSpec((blk, HEAD_DIM), lambda b, h, i: (b * tiles + i, h)),
            pl.BlockSpec((SEQ, HEAD_DIM), lambda b, h, i: (b, h)),
            pl.BlockSpec((SEQ, HEAD_DIM), lambda b, h, i: (b, N_HEADS + h)),
            pl.BlockSpec((SMALL_BLOCK, HEAD_DIM), lambda b, h, i: (meta_blk, h)),
            pl.BlockSpec((SMALL_BLOCK, HEAD_DIM), lambda b, h, i: (meta_blk, N_HEADS + h))],
        out_specs=pl.BlockSpec((blk, HEAD_DIM), lambda b, h, i: (b * tiles + i, h)),
        compiler_params=_params(3),
        name="attn_prompt",
    )(q, kv, kv, kv, kv)


ATT_SAMPLE_HEADS = 4


def _attn_sample_kernel(q_ref, kn_ref, vn_ref, kc_ref, vc_ref, km_ref, vm_ref, o_ref):
    blk = ATT_BLOCK
    upper_small = _upper(SMALL_BLOCK)
    upper_blk = _upper(blk)
    new_mask = _strict_lower_mask(DEC_SEQ, SMALL_BLOCK)
    meta_mask = _first_keys_mask(DEC_SEQ, SMALL_BLOCK, N_META)
    fill = jnp.zeros((SMALL_BLOCK - DEC_SEQ, HEAD_DIM), BF16)
    for hh in range(ATT_SAMPLE_HEADS):
        cols = slice(hh * HEAD_DIM, (hh + 1) * HEAD_DIM)
        q = q_ref[:, cols]
        k_new = jnp.concatenate([kn_ref[:, cols], fill], axis=0)
        v_new = jnp.concatenate([vn_ref[:, cols], fill], axis=0)
        acc, carry = _sb_block(q, k_new, v_new, jnp.zeros((DEC_SEQ, 1), F32),
                               upper_small, new_mask)
        for c in reversed(range(PAST_LEN // blk)):
            rows = slice(c * blk, (c + 1) * blk)
            out, carry = _sb_block(q, kc_ref[rows, cols].astype(BF16),
                                   vc_ref[rows, cols].astype(BF16), carry, upper_blk, None)
            acc = acc + out
        out, _ = _sb_block(q, km_ref[:, cols], vm_ref[:, cols], carry, upper_small, meta_mask)
        o_ref[:, cols] = (acc + out).astype(o_ref.dtype)


def _attn_sample(q, kv, cache_k, cache_v):
    hw = ATT_SAMPLE_HEADS * HEAD_DIM
    hb = N_HEADS // ATT_SAMPLE_HEADS
    row_blk0 = SAMPLE_ROW0 // DEC_SEQ
    meta_blk = META_ROW0 // SMALL_BLOCK
    new = lambda off: pl.BlockSpec((DEC_SEQ, hw), lambda s, h: (row_blk0 + s, off + h))
    meta = lambda off: pl.BlockSpec((SMALL_BLOCK, hw), lambda s, h: (meta_blk, off + h))
    cache = pl.BlockSpec((None, PAST_LEN, hw), lambda s, h: (s, 0, h))
    return pl.pallas_call(
        _attn_sample_kernel,
        out_shape=jax.ShapeDtypeStruct((N_SAMPLE, D_MODEL), BF16),
        grid=(DEC_BATCH, hb),
        in_specs=[new(0), new(0), new(hb), cache, cache, meta(0), meta(hb)],
        out_specs=pl.BlockSpec((DEC_SEQ, hw), lambda s, h: (s, h)),
        compiler_params=_params(2),
        name="attn_sample",
    )(q, kv, kv, cache_k.reshape(DEC_BATCH, PAST_LEN, D_MODEL),
      cache_v.reshape(DEC_BATCH, PAST_LEN, D_MODEL), kv, kv)


def _ffn(x, xb, l, wg, wu, wd, ln_g, ln_b, out_dtypes):
    h = _swiglu(xb, wg, wu)
    (y,) = _matmul(h, wd, [F32], 512, "ffn_down")
    zero = jnp.zeros((1, D_MODEL), F32)
    return _add_ln(x, y, zero, ln_g[l][None], ln_b[l][None], out_dtypes, "ffn_ln")


def kernel(x_prompt, x_sample, state_conv, cache_k, cache_v, meta, a_w_pw1, a_b_pw1, a_w_dw, a_b_dw, a_ln_g, a_ln_b, a_w_pw2, a_b_pw2, w_kv, w_q, w_o, ln_mix_g, ln_mix_b, ln_ffn_g, ln_ffn_b, w_gate, w_up, w_down):
    d = D_MODEL
    ff_pad = D_FF_PAD - D_FF
    w_gate_b = jnp.pad(w_gate.astype(BF16), ((0, 0), (0, 0), (0, ff_pad)))
    w_up_b = jnp.pad(w_up.astype(BF16), ((0, 0), (0, 0), (0, ff_pad)))
    w_down_b = jnp.pad(w_down.astype(BF16), ((0, 0), (0, ff_pad), (0, 0)))
    zero_vec = jnp.zeros((1, d), F32)

    x0 = jnp.concatenate([
        x_prompt.reshape(N_PROMPT, d), x_sample.reshape(N_SAMPLE, d), meta,
        jnp.zeros((N_ROWS - META_ROW0 - N_META, d), F32)], axis=0)

    g = _conv_glu(x0.astype(BF16), a_w_pw1[0].astype(BF16), a_b_pw1[0][None])
    g_sample = g[SAMPLE_ROW0:META_ROW0].reshape(DEC_BATCH, DEC_SEQ, d)
    g_meta = g[META_ROW0:META_ROW0 + N_META]
    first = jnp.concatenate([jnp.zeros((CONV_HALO - N_META, d), F32), g_meta], axis=0)
    conv_args = (a_w_dw[0], a_b_dw[0][None], a_ln_g[0][None], a_ln_b[0][None])
    c_prompt = _conv_prompt(g, first, *conv_args)
    hist = jnp.concatenate([state_conv[0], jnp.zeros((1, CONV_W - 1, d), F32)], axis=0)
    new = jnp.concatenate([g_sample, g_meta[None]], axis=0)
    win = jnp.concatenate([jnp.zeros((DEC_BATCH + 1, CONV_LEAD, d), F32), hist, new], axis=1)
    c_tail = _conv_streams(win, *conv_args)
    c = jnp.concatenate([c_prompt, c_tail,
                         jnp.zeros((N_ROWS - META_ROW0 - N_META, d), BF16)], axis=0)
    (m,) = _matmul(c, a_w_pw2[0].astype(BF16), [F32], 2048, "conv_pw2")
    x1, x1b = _add_ln(x0, m, a_b_pw2[0][None], ln_mix_g[0][None], ln_mix_b[0][None],
                      [F32, BF16], "mix_ln")
    x2, x2b = _ffn(x1, x1b, 0, w_gate_b[0], w_up_b[0], w_down_b[0], ln_ffn_g, ln_ffn_b,
                   [F32, BF16])

    kv, kvb = _matmul(x2b, w_kv.astype(BF16), [F32, BF16], 1024, "kv_proj")
    (qb,) = _matmul(x2b, w_q[0].astype(BF16), [BF16], 2048, "q_proj")
    o_prompt = _attn_prompt(qb, kvb)
    o_sample = _attn_sample(qb, kvb, cache_k, cache_v)
    o = jnp.concatenate([o_prompt, o_sample,
                         jnp.zeros((N_ROWS - META_ROW0, d), BF16)], axis=0)
    (m,) = _matmul(o, w_o[0].astype(BF16), [F32], 2048, "o_proj")
    x3, x3b = _add_ln(x2, m, zero_vec, ln_mix_g[1][None], ln_mix_b[1][None],
                      [F32, BF16], "mix_ln")
    (x4,) = _ffn(x3, x3b, 1, w_gate_b[1], w_up_b[1], w_down_b[1], ln_ffn_g, ln_ffn_b, [F32])

    y_prompt = x4[:N_PROMPT].reshape(BATCH, SEQ, d)
    y_sample = x4[SAMPLE_ROW0:META_ROW0].reshape(DEC_BATCH, DEC_SEQ, d)
    g_prompt = g[:N_PROMPT].reshape(BATCH, SEQ, d)
    state_conv_prompt = g_prompt[:, SEQ - (CONV_W - 1):][None]
    state_conv_sample = jnp.concatenate([state_conv[0][:, DEC_SEQ:], g_sample], axis=1)[None]

    def heads(a):
        return a.reshape(a.shape[:-1] + (N_HEADS, HEAD_DIM))

    def prompt_rows(a):
        a_meta = jnp.broadcast_to(a[META_ROW0:META_ROW0 + N_META][None], (BATCH, N_META, d))
        return heads(jnp.concatenate([a_meta, a[:N_PROMPT].reshape(BATCH, SEQ, d)], axis=1))

    def sample_rows(a):
        return heads(a[SAMPLE_ROW0:META_ROW0].reshape(DEC_BATCH, DEC_SEQ, d))

    k_all, v_all = kv[:, :d], kv[:, d:]
    return (y_prompt, y_sample, state_conv_prompt, state_conv_sample,
            prompt_rows(k_all), prompt_rows(v_all), sample_rows(k_all), sample_rows(v_all))
```

```python
import functools

import jax
import jax.numpy as jnp
from jax import lax
from jax.experimental import pallas as pl
from jax.experimental.pallas import tpu as pltpu

D_MODEL = 4096
BATCH = 2
SEQ = 4096
DEC_BATCH = 16
DEC_SEQ = 16
PAST_LEN = 1024
N_META = 16
CONV_W = 31
HEAD_DIM = 128
N_HEADS = D_MODEL // HEAD_DIM
D_FF = 11008
DEPTH = 2
ALPHA = (2.0 * DEPTH) ** 0.25
LN_EPS = 1e-5
SCALE = HEAD_DIM ** -0.5

N_PROMPT = BATCH * SEQ
N_SAMPLE = DEC_BATCH * DEC_SEQ
ROW_TILE = 512
N_ROWS = 8704
SAMPLE_ROW0 = N_PROMPT
META_ROW0 = N_PROMPT + N_SAMPLE
D_FF_PAD = 11264
CONV_HALO = 32
CONV_LEAD = CONV_HALO - (CONV_W - 1)
ATT_BLOCK = 256
SMALL_BLOCK = 128
VMEM_LIMIT = 56 * 1024 * 1024

F32 = jnp.float32
BF16 = jnp.bfloat16


def _params(n_axes):
    return pltpu.CompilerParams(
        dimension_semantics=("arbitrary",) * n_axes, vmem_limit_bytes=VMEM_LIMIT)


def _mm_kernel(x_ref, w_ref, *o_refs):
    acc = jnp.dot(x_ref[...], w_ref[...], preferred_element_type=F32)
    for o_ref in o_refs:
        o_ref[...] = acc.astype(o_ref.dtype)


def _matmul(x, w, out_dtypes, bn, name):
    m, k = x.shape
    n = w.shape[1]
    bm = ROW_TILE
    outs = pl.pallas_call(
        _mm_kernel,
        out_shape=[jax.ShapeDtypeStruct((m, n), dt) for dt in out_dtypes],
        grid=(n // bn, m // bm),
        in_specs=[pl.BlockSpec((bm, k), lambda j, i: (i, 0)),
                  pl.BlockSpec((k, bn), lambda j, i: (0, j))],
        out_specs=[pl.BlockSpec((bm, bn), lambda j, i: (i, j)) for _ in out_dtypes],
        compiler_params=_params(2),
        name=name,
    )(x, w)
    return outs


def _glu_kernel(x_ref, wa_ref, wb_ref, ba_ref, bb_ref, o_ref):
    x = x_ref[...]
    a = jnp.dot(x, wa_ref[...], preferred_element_type=F32) + ba_ref[...]
    b = jnp.dot(x, wb_ref[...], preferred_element_type=F32) + bb_ref[...]
    o_ref[...] = (a * jax.nn.sigmoid(b)).astype(o_ref.dtype)


def _swiglu_kernel(x_ref, wg_ref, wu_ref, o_ref):
    x = x_ref[...]
    g = jnp.dot(x, wg_ref[...], preferred_element_type=F32)
    u = jnp.dot(x, wu_ref[...], preferred_element_type=F32)
    o_ref[...] = (g * jax.nn.sigmoid(g) * u).astype(o_ref.dtype)


def _conv_glu(x, w1, b1):
    m, k = x.shape
    n = w1.shape[1] // 2
    bm, bn = ROW_TILE, 1024
    nb = n // bn
    return pl.pallas_call(
        _glu_kernel,
        out_shape=jax.ShapeDtypeStruct((m, n), F32),
        grid=(nb, m // bm),
        in_specs=[pl.BlockSpec((bm, k), lambda j, i: (i, 0)),
                  pl.BlockSpec((k, bn), lambda j, i: (0, j)),
                  pl.BlockSpec((k, bn), lambda j, i: (0, j + nb)),
                  pl.BlockSpec((1, bn), lambda j, i: (0, j)),
                  pl.BlockSpec((1, bn), lambda j, i: (0, j + nb))],
        out_specs=pl.BlockSpec((bm, bn), lambda j, i: (i, j)),
        compiler_params=_params(2),
        name="conv_pw1_glu",
    )(x, w1, w1, b1, b1)


def _swiglu(x, wg, wu):
    m, k = x.shape
    n = wg.shape[1]
    bm, bn = ROW_TILE, 1024
    return pl.pallas_call(
        _swiglu_kernel,
        out_shape=jax.ShapeDtypeStruct((m, n), BF16),
        grid=(n // bn, m // bm),
        in_specs=[pl.BlockSpec((bm, k), lambda j, i: (i, 0)),
                  pl.BlockSpec((k, bn), lambda j, i: (0, j)),
                  pl.BlockSpec((k, bn), lambda j, i: (0, j))],
        out_specs=pl.BlockSpec((bm, bn), lambda j, i: (i, j)),
        compiler_params=_params(2),
        name="ffn_gate_up",
    )(x, wg, wu)


LN_ROWS = 16


def _layer_norm_rows(x, g, b):
    mu = jnp.mean(x, axis=-1, keepdims=True)
    xc = x - mu
    var = jnp.mean(xc * xc, axis=-1, keepdims=True)
    return xc * lax.rsqrt(var + LN_EPS) * g + b


def _add_ln_kernel(r_ref, m_ref, bias_ref, g_ref, b_ref, *o_refs):
    n_chunks = r_ref.shape[0] // LN_ROWS

    def body(c, carry):
        rows = pl.ds(pl.multiple_of(c * LN_ROWS, LN_ROWS), LN_ROWS)
        x = ALPHA * r_ref[rows, :] + (m_ref[rows, :] + bias_ref[...])
        y = _layer_norm_rows(x, g_ref[...], b_ref[...])
        for o_ref in o_refs:
            o_ref[rows, :] = y.astype(o_ref.dtype)
        return carry

    lax.fori_loop(0, n_chunks, body, 0)


def _add_ln(r, m, bias, g, b, out_dtypes, name):
    rows, d = r.shape
    tm = 256
    row_spec = pl.BlockSpec((tm, d), lambda i: (i, 0))
    vec_spec = pl.BlockSpec((1, d), lambda i: (0, 0))
    return pl.pallas_call(
        _add_ln_kernel,
        out_shape=[jax.ShapeDtypeStruct((rows, d), dt) for dt in out_dtypes],
        grid=(rows // tm,),
        in_specs=[row_spec, row_spec, vec_spec, vec_spec, vec_spec],
        out_specs=[row_spec for _ in out_dtypes],
        compiler_params=_params(1),
        name=name,
    )(r, m, bias, g, b)


CONV_ROWS = 64
CONV_COLS = 512


def _conv_ln_silu(win_ref, wdw_ref, bdw_ref, g_ref, b_ref, o_ref, c_ref, t_rows):
    d = win_ref.shape[1]
    rh = min(CONV_ROWS, t_rows)

    def col_body(cc, carry):
        cols = pl.ds(pl.multiple_of(cc * CONV_COLS, CONV_COLS), CONV_COLS)
        for r0 in range(0, t_rows, rh):
            acc = jnp.zeros((rh, CONV_COLS), F32) + bdw_ref[:, cols]
            for k in range(CONV_W):
                r = r0 + CONV_LEAD + k
                acc = acc + wdw_ref[k:k + 1, cols] * win_ref[r:r + rh, cols]
            c_ref[r0:r0 + rh, cols] = acc
        return carry

    lax.fori_loop(0, d // CONV_COLS, col_body, 0)

    def ln_body(c, carry):
        rows = pl.ds(pl.multiple_of(c * LN_ROWS, LN_ROWS), LN_ROWS)
        y = _layer_norm_rows(c_ref[rows, :], g_ref[...], b_ref[...])
        o_ref[rows, :] = (y * jax.nn.sigmoid(y)).astype(o_ref.dtype)
        return carry

    lax.fori_loop(0, t_rows // LN_ROWS, ln_body, 0)


def _conv_prompt_kernel(prev_ref, cur_ref, first_ref, wdw_ref, bdw_ref, g_ref, b_ref,
                        o_ref, win_ref, c_ref):
    t_rows = cur_ref.shape[0]
    i = pl.program_id(1)

    @pl.when(i == 0)
    def _():
        win_ref[0:CONV_HALO, :] = first_ref[...]

    @pl.when(i > 0)
    def _():
        win_ref[0:CONV_HALO, :] = prev_ref[...]

    win_ref[CONV_HALO:CONV_HALO + t_rows, :] = cur_ref[...]
    _conv_ln_silu(win_ref, wdw_ref, bdw_ref, g_ref, b_ref, o_ref, c_ref, t_rows)


def _conv_stream_kernel(win_ref, wdw_ref, bdw_ref, g_ref, b_ref, o_ref, c_ref):
    _conv_ln_silu(win_ref, wdw_ref, bdw_ref, g_ref, b_ref, o_ref, c_ref, o_ref.shape[0])


def _conv_prompt(g, first, wdw, bdw, ln_g, ln_b):
    d = g.shape[1]
    t = 256
    tiles = SEQ // t
    halo_per_tile = t // CONV_HALO
    vec = lambda r: pl.BlockSpec((r, d), lambda b, i: (0, 0))
    return pl.pallas_call(
        _conv_prompt_kernel,
        out_shape=jax.ShapeDtypeStruct((N_PROMPT, d), BF16),
        grid=(BATCH, tiles),
        in_specs=[
            pl.BlockSpec((CONV_HALO, d),
                         lambda b, i: (jnp.maximum((b * tiles + i) * halo_per_tile - 1, 0), 0)),
            pl.BlockSpec((t, d), lambda b, i: (b * tiles + i, 0)),
            vec(CONV_HALO), vec(CONV_W), vec(1), vec(1), vec(1)],
        out_specs=pl.BlockSpec((t, d), lambda b, i: (b * tiles + i, 0)),
        scratch_shapes=[pltpu.VMEM((CONV_HALO + t, d), F32), pltpu.VMEM((t, d), F32)],
        compiler_params=_params(2),
        name="conv_prompt",
    )(g, g, first, wdw, bdw, ln_g, ln_b)


def _conv_streams(win, wdw, bdw, ln_g, ln_b):
    s, rows, d = win.shape
    t = rows - CONV_HALO
    vec = lambda r: pl.BlockSpec((r, d), lambda i: (0, 0))
    return pl.pallas_call(
        _conv_stream_kernel,
        out_shape=jax.ShapeDtypeStruct((s * t, d), BF16),
        grid=(s,),
        in_specs=[pl.BlockSpec((None, rows, d), lambda i: (i, 0, 0)),
                  vec(CONV_W), vec(1), vec(1), vec(1)],
        out_specs=pl.BlockSpec((t, d), lambda i: (i, 0)),
        scratch_shapes=[pltpu.VMEM((t, d), F32)],
        compiler_params=_params(1),
        name="conv_streams",
    )(win, wdw, bdw, ln_g, ln_b)


LOG2E = 1.4426950408889634
SCORE_SCALE = SCALE * LOG2E
ATT_PROMPT_HEADS = 4
ATT_SAMPLE_HEADS = 8
SIGN_BIT = 0x80000000
BF16_BITS = 0xFFFF0000


def _suffix_sum_matrix(n):
    r = lax.broadcasted_iota(jnp.int32, (2 * n, n), 0) % n
    c = lax.broadcasted_iota(jnp.int32, (2 * n, n), 1)
    return jnp.where(r > c, 1.0, 0.0).astype(BF16)


def _bits_op(x, op, bits):
    u = lax.bitcast_convert_type(x, jnp.uint32)
    return lax.bitcast_convert_type(op(u, jnp.uint32(bits)), F32)


def _sb_terms(s, mask):
    z2 = s * SCORE_SCALE
    neg_abs = _bits_op(z2, jnp.bitwise_or, SIGN_BIT)
    t2 = jnp.log(1.0 + jnp.exp2(neg_abs)) * LOG2E
    log_beta = jnp.minimum(z2, 0.0) - t2
    neg_log_keep = jnp.maximum(z2, 0.0) + t2
    if mask is not None:
        neg_log_keep = jnp.where(mask, neg_log_keep, 0.0)
    hi = _bits_op(neg_log_keep, jnp.bitwise_and, BF16_BITS)
    lo = neg_log_keep - hi
    hilo = jnp.concatenate([hi.astype(BF16), lo.astype(BF16)], axis=1)
    return log_beta, neg_log_keep[:, 0:1], hilo


def _sb_finish(log_beta, first_col, after, mask):
    a = jnp.exp2(log_beta - after)
    if mask is not None:
        a = jnp.where(mask, a, 0.0)
    return a.astype(BF16), after[:, 0:1] + first_col


def _sb_sweep(state, blocks):
    state = list(state)
    n = len(blocks)
    scores, mid, outs = {}, {}, {}
    for t in range(n + 3):
        if t < n:
            scores[t] = blocks[t][1]()
        g = t - 1
        if 0 <= g < n:
            _, _, _, upper2, mask = blocks[g]
            log_beta, first_col, hilo = _sb_terms(scores.pop(g), mask)
            mid[g] = (log_beta, first_col, jnp.dot(hilo, upper2, preferred_element_type=F32))
        g = t - 2
        if 0 <= g < n:
            a, total = _sb_finish(*mid.pop(g), blocks[g][4])
            outs[g] = (blocks[g][2](a), total)
        g = t - 3
        if 0 <= g < n:
            out, total = outs.pop(g)
            acc, carry = state[blocks[g][0]]
            state[blocks[g][0]] = (acc + out * jnp.exp2(-carry), carry + total)
    return state


def _scores(q, k):
    return lax.dot_general(q, k, (((1,), (1,)), ((), ())), preferred_element_type=F32)


def _strict_lower_mask(nq, nk, period):
    r = lax.broadcasted_iota(jnp.int32, (nq, nk), 0) % period
    c = lax.broadcasted_iota(jnp.int32, (nq, nk), 1)
    return c < r


def _first_keys_mask(nq, nk, n_valid):
    return lax.broadcasted_iota(jnp.int32, (nq, nk), 1) < n_valid


def _attn_prompt_kernel(q_ref, k_ref, v_ref, km_ref, vm_ref, u_ref, us_ref, o_ref):
    qi = pl.program_id(2)
    blk = ATT_BLOCK
    cols = [slice(h * HEAD_DIM, (h + 1) * HEAD_DIM) for h in range(ATT_PROMPT_HEADS)]

    def head_blocks(kr, vr, rows, u, mask):
        def block(h):
            return (h,
                    lambda: _scores(q_ref[:, cols[h]], kr[rows, cols[h]]),
                    lambda a: jnp.dot(a, vr[rows, cols[h]], preferred_element_type=F32),
                    u[...], mask)
        return [block(h) for h in range(len(cols))]

    zero = [(jnp.zeros((blk, HEAD_DIM), F32), jnp.zeros((blk, 1), F32)) for _ in cols]
    diag = pl.ds(pl.multiple_of(qi * blk, blk), blk)
    state = _sb_sweep(zero, head_blocks(k_ref, v_ref, diag, u_ref,
                                        _strict_lower_mask(blk, blk, blk)))

    def body(step, flat):
        rows = pl.ds(pl.multiple_of((qi - 1 - step) * blk, blk), blk)
        state = [(flat[2 * h], flat[2 * h + 1]) for h in range(len(cols))]
        state = _sb_sweep(state, head_blocks(k_ref, v_ref, rows, u_ref, None))
        return tuple(x for pair in state for x in pair)

    flat = lax.fori_loop(0, qi, body, tuple(x for pair in state for x in pair))
    state = [(flat[2 * h], flat[2 * h + 1]) for h in range(len(cols))]
    state = _sb_sweep(state, head_blocks(km_ref, vm_ref, slice(None), us_ref,
                                         _first_keys_mask(blk, SMALL_BLOCK, N_META)))
    for h, (acc, _) in enumerate(state):
        o_ref[:, cols[h]] = acc.astype(o_ref.dtype)


def _attn_prompt(q, kv, upper2, upper2_small):
    blk = ATT_BLOCK
    tiles = SEQ // blk
    hw = ATT_PROMPT_HEADS * HEAD_DIM
    hb = N_HEADS // ATT_PROMPT_HEADS
    meta_blk = META_ROW0 // SMALL_BLOCK
    const = lambda a: pl.BlockSpec(a.shape, lambda b, h, i: (0, 0))
    return pl.pallas_call(
        _attn_prompt_kernel,
        out_shape=jax.ShapeDtypeStruct((N_PROMPT, D_MODEL), BF16),
        grid=(BATCH, hb, tiles),
        in_specs=[
            pl.BlockSpec((blk, hw), lambda b, h, i: (b * tiles + i, h)),
            pl.BlockSpec((SEQ, hw), lambda b, h, i: (b, h)),
            pl.BlockSpec((SEQ, hw), lambda b, h, i: (b, hb + h)),
            pl.BlockSpec((SMALL_BLOCK, hw), lambda b, h, i: (meta_blk, h)),
            pl.BlockSpec((SMALL_BLOCK, hw), lambda b, h, i: (meta_blk, hb + h)),
            const(upper2), const(upper2_small)],
        out_specs=pl.BlockSpec((blk, hw), lambda b, h, i: (b * tiles + i, h)),
        compiler_params=_params(3),
        name="attn_prompt",
    )(q, kv, kv, kv, kv, upper2, upper2_small)


def _attn_sample_kernel(q_ref, kn_ref, vn_ref, kc_ref, vc_ref, km_ref, vm_ref, u_ref, us_ref,
                        o_ref):
    blk = ATT_BLOCK
    nh = ATT_SAMPLE_HEADS
    nq = nh * DEC_SEQ
    cols = [slice(h * HEAD_DIM, (h + 1) * HEAD_DIM) for h in range(nh)]
    qrows = [slice(h * DEC_SEQ, (h + 1) * DEC_SEQ) for h in range(nh)]
    fill = jnp.zeros((SMALL_BLOCK - DEC_SEQ, HEAD_DIM), BF16)
    kc2 = kc_ref.reshape(PAST_LEN * nh, HEAD_DIM)
    vc2 = vc_ref.reshape(PAST_LEN * nh, HEAD_DIM)

    def block(k_of, v_of, u, mask):
        return (0,
                lambda: jnp.concatenate(
                    [_scores(q_ref[:, cols[h]], k_of(h)) for h in range(nh)], axis=0),
                lambda a: jnp.concatenate(
                    [jnp.dot(a[qrows[h]], v_of(h), preferred_element_type=F32)
                     for h in range(nh)], axis=0),
                u[...], mask)

    def cached(ref2, c):
        return lambda h: ref2[pl.ds(c * blk * nh + h, blk, stride=nh), :].astype(BF16)

    blocks = [block(lambda h: jnp.concatenate([kn_ref[:, cols[h]], fill], axis=0),
                    lambda h: jnp.concatenate([vn_ref[:, cols[h]], fill], axis=0),
                    us_ref, _strict_lower_mask(nq, SMALL_BLOCK, DEC_SEQ))]
    for c in reversed(range(PAST_LEN // blk)):
        blocks.append(block(cached(kc2, c), cached(vc2, c), u_ref, None))
    blocks.append(block(lambda h: km_ref[:, cols[h]], lambda h: vm_ref[:, cols[h]],
                        us_ref, _first_keys_mask(nq, SMALL_BLOCK, N_META)))
    zero = [(jnp.zeros((nq, HEAD_DIM), F32), jnp.zeros((nq, 1), F32))]
    ((acc, _),) = _sb_sweep(zero, blocks)
    for h in range(nh):
        o_ref[:, cols[h]] = acc[qrows[h]].astype(o_ref.dtype)


def _attn_sample(q, kv, cache_k, cache_v, upper2, upper2_small):
    nh = ATT_SAMPLE_HEADS
    hw = nh * HEAD_DIM
    hb = N_HEADS // nh
    row_blk0 = SAMPLE_ROW0 // DEC_SEQ
    meta_blk = META_ROW0 // SMALL_BLOCK
    new = lambda off: pl.BlockSpec((DEC_SEQ, hw), lambda s, h: (row_blk0 + s, off + h))
    meta = lambda off: pl.BlockSpec((SMALL_BLOCK, hw), lambda s, h: (meta_blk, off + h))
    cache = pl.BlockSpec((None, PAST_LEN, nh, HEAD_DIM), lambda s, h: (s, 0, h, 0))
    const = lambda a: pl.BlockSpec(a.shape, lambda s, h: (0, 0))
    return pl.pallas_call(
        _attn_sample_kernel,
        out_shape=jax.ShapeDtypeStruct((N_SAMPLE, D_MODEL), BF16),
        grid=(DEC_BATCH, hb),
        in_specs=[new(0), new(0), new(hb), cache, cache, meta(0), meta(hb),
                  const(upper2), const(upper2_small)],
        out_specs=pl.BlockSpec((DEC_SEQ, hw), lambda s, h: (s, h)),
        compiler_params=_params(2),
        name="attn_sample",
    )(q, kv, kv, cache_k, cache_v, kv, kv, upper2, upper2_small)


def _ffn(x, xb, l, wg, wu, wd, ln_g, ln_b, out_dtypes):
    h = _swiglu(xb, wg, wu)
    (y,) = _matmul(h, wd, [F32], 512, "ffn_down")
    zero = jnp.zeros((1, D_MODEL), F32)
    return _add_ln(x, y, zero, ln_g[l][None], ln_b[l][None], out_dtypes, "ffn_ln")


def kernel(x_prompt, x_sample, state_conv, cache_k, cache_v, meta, a_w_pw1, a_b_pw1, a_w_dw, a_b_dw, a_ln_g, a_ln_b, a_w_pw2, a_b_pw2, w_kv, w_q, w_o, ln_mix_g, ln_mix_b, ln_ffn_g, ln_ffn_b, w_gate, w_up, w_down):
    d = D_MODEL
    ff_pad = D_FF_PAD - D_FF
    w_gate_b = jnp.pad(w_gate.astype(BF16), ((0, 0), (0, 0), (0, ff_pad)))
    w_up_b = jnp.pad(w_up.astype(BF16), ((0, 0), (0, 0), (0, ff_pad)))
    w_down_b = jnp.pad(w_down.astype(BF16), ((0, 0), (0, ff_pad), (0, 0)))
    zero_vec = jnp.zeros((1, d), F32)

    x0 = jnp.concatenate([
        x_prompt.reshape(N_PROMPT, d), x_sample.reshape(N_SAMPLE, d), meta,
        jnp.zeros((N_ROWS - META_ROW0 - N_META, d), F32)], axis=0)

    g = _conv_glu(x0.astype(BF16), a_w_pw1[0].astype(BF16), a_b_pw1[0][None])
    g_sample = g[SAMPLE_ROW0:META_ROW0].reshape(DEC_BATCH, DEC_SEQ, d)
    g_meta = g[META_ROW0:META_ROW0 + N_META]
    first = jnp.concatenate([jnp.zeros((CONV_HALO - N_META, d), F32), g_meta], axis=0)
    conv_args = (a_w_dw[0], a_b_dw[0][None], a_ln_g[0][None], a_ln_b[0][None])
    c_prompt = _conv_prompt(g, first, *conv_args)
    hist = jnp.concatenate([state_conv[0], jnp.zeros((1, CONV_W - 1, d), F32)], axis=0)
    new = jnp.concatenate([g_sample, g_meta[None]], axis=0)
    win = jnp.concatenate([jnp.zeros((DEC_BATCH + 1, CONV_LEAD, d), F32), hist, new], axis=1)
    c_tail = _conv_streams(win, *conv_args)
    c = jnp.concatenate([c_prompt, c_tail,
                         jnp.zeros((N_ROWS - META_ROW0 - N_META, d), BF16)], axis=0)
    (m,) = _matmul(c, a_w_pw2[0].astype(BF16), [F32], 2048, "conv_pw2")
    x1, x1b = _add_ln(x0, m, a_b_pw2[0][None], ln_mix_g[0][None], ln_mix_b[0][None],
                      [F32, BF16], "mix_ln")
    x2, x2b = _ffn(x1, x1b, 0, w_gate_b[0], w_up_b[0], w_down_b[0], ln_ffn_g, ln_ffn_b,
                   [F32, BF16])

    kv, kvb = _matmul(x2b, w_kv.astype(BF16), [F32, BF16], 1024, "kv_proj")
    (qb,) = _matmul(x2b, w_q[0].astype(BF16), [BF16], 2048, "q_proj")
    upper2 = _suffix_sum_matrix(ATT_BLOCK)
    upper2_small = _suffix_sum_matrix(SMALL_BLOCK)
    o_prompt = _attn_prompt(qb, kvb, upper2, upper2_small)
    o_sample = _attn_sample(qb, kvb, cache_k, cache_v, upper2, upper2_small)
    o = jnp.concatenate([o_prompt, o_sample,
                         jnp.zeros((N_ROWS - META_ROW0, d), BF16)], axis=0)
    (m,) = _matmul(o, w_o[0].astype(BF16), [F32], 2048, "o_proj")
    x3, x3b = _add_ln(x2, m, zero_vec, ln_mix_g[1][None], ln_mix_b[1][None],
                      [F32, BF16], "mix_ln")
    (x4,) = _ffn(x3, x3b, 1, w_gate_b[1], w_up_b[1], w_down_b[1], ln_ffn_g, ln_ffn_b, [F32])

    y_prompt = x4[:N_PROMPT].reshape(BATCH, SEQ, d)
    y_sample = x4[SAMPLE_ROW0:META_ROW0].reshape(DEC_BATCH, DEC_SEQ, d)
    g_prompt = g[:N_PROMPT].reshape(BATCH, SEQ, d)
    state_conv_prompt = g_prompt[:, SEQ - (CONV_W - 1):][None]
    state_conv_sample = jnp.concatenate([state_conv[0][:, DEC_SEQ:], g_sample], axis=1)[None]

    def heads(a):
        return a.reshape(a.shape[:-1] + (N_HEADS, HEAD_DIM))

    def prompt_rows(a):
        a_meta = jnp.broadcast_to(a[META_ROW0:META_ROW0 + N_META][None], (BATCH, N_META, d))
        return heads(jnp.concatenate([a_meta, a[:N_PROMPT].reshape(BATCH, SEQ, d)], axis=1))

    def sample_rows(a):
        return heads(a[SAMPLE_ROW0:META_ROW0].reshape(DEC_BATCH, DEC_SEQ, d))

    k_all, v_all = kv[:, :d], kv[:, d:]
    return (y_prompt, y_sample, state_conv_prompt, state_conv_sample,
            prompt_rows(k_all), prompt_rows(v_all), sample_rows(k_all), sample_rows(v_all))
```

```python
import functools

import jax
import jax.numpy as jnp
from jax import lax
from jax.experimental import pallas as pl
from jax.experimental.pallas import tpu as pltpu

D_MODEL = 4096
BATCH = 2
SEQ = 4096
DEC_BATCH = 16
DEC_SEQ = 16
PAST_LEN = 1024
N_META = 16
CONV_W = 31
HEAD_DIM = 128
D_FF = 11008
DEPTH = 2
LN_EPS = 1e-5

N_HEADS = D_MODEL // HEAD_DIM
ALPHA = (2.0 * DEPTH) ** 0.25
SCALE = HEAD_DIM ** -0.5
LOG2E = 1.4426950408889634
SCORE_SCALE = SCALE * LOG2E

LANES = 128
SUBLANES = 8
VMEM_LIMIT = 56 * 1024 * 1024
ROW_TILE = 512
COL_TILE = 512
DOWN_COL_TILE = 512
CAST_ROWS = 256
LN_TILE = 256
LN_ROWS = 16
CONV_TILE = 256
CONV_ROWS = 64
CONV_COLS = 512
CONV_HALO = 32
ATT_BLOCK = 256
SMALL_BLOCK = 128
ATT_PROMPT_HEADS = 8
ATT_SAMPLE_HEADS = SUBLANES

N_PROMPT = BATCH * SEQ
N_SAMPLE = DEC_BATCH * DEC_SEQ
SAMPLE_ROW0 = N_PROMPT
META_ROW0 = N_PROMPT + N_SAMPLE
N_ROWS = -(-(META_ROW0 + N_META) // ROW_TILE) * ROW_TILE
N_TAIL = N_ROWS - N_PROMPT
D_FF_PAD = -(-D_FF // COL_TILE) * COL_TILE
CONV_LEAD = CONV_HALO - (CONV_W - 1)
SIGN_BIT = 0x80000000
BF16_BITS = 0xFFFF0000
NEG_BIG = -1e30

assert N_PROMPT % ROW_TILE == 0 and N_TAIL % ROW_TILE == 0
assert N_PROMPT % LN_TILE == 0 and N_SAMPLE == LN_TILE
assert META_ROW0 % SMALL_BLOCK == 0 and SEQ % ATT_BLOCK == 0 and PAST_LEN % ATT_BLOCK == 0

F32 = jnp.float32
BF16 = jnp.bfloat16


def _params(n_axes):
    return pltpu.CompilerParams(
        dimension_semantics=("arbitrary",) * n_axes, vmem_limit_bytes=VMEM_LIMIT)


def _cast_weight(w_ref, wb_ref, n_cols):
    k, bn = wb_ref.shape
    if n_cols < bn:
        wb_ref[:, n_cols:] = jnp.zeros((k, bn - n_cols), BF16)

    def body(c, carry):
        rows = pl.ds(pl.multiple_of(c * CAST_ROWS, CAST_ROWS), CAST_ROWS)
        wb_ref[rows, :n_cols] = w_ref[rows, :n_cols].astype(BF16)
        return carry

    lax.fori_loop(0, k // CAST_ROWS, body, 0)


def _cast_weights_once(w_refs, wb_refs, n_valid_last):
    j, i = pl.program_id(0), pl.program_id(1)
    last = pl.num_programs(0) - 1
    bn = wb_refs[0].shape[1]
    if n_valid_last == bn:
        @pl.when(i == 0)
        def _():
            for w_ref, wb_ref in zip(w_refs, wb_refs):
                _cast_weight(w_ref, wb_ref, bn)
    else:
        @pl.when((i == 0) & (j < last))
        def _():
            for w_ref, wb_ref in zip(w_refs, wb_refs):
                _cast_weight(w_ref, wb_ref, bn)

        @pl.when((i == 0) & (j == last))
        def _():
            for w_ref, wb_ref in zip(w_refs, wb_refs):
                _cast_weight(w_ref, wb_ref, n_valid_last)


def _mm_cast_kernel(x_ref, w_ref, *refs):
    *o_refs, wb_ref = refs
    _cast_weights_once([w_ref], [wb_ref], wb_ref.shape[1])
    acc = jnp.dot(x_ref[...], wb_ref[...], preferred_element_type=F32)
    for o_ref in o_refs:
        o_ref[...] = acc.astype(o_ref.dtype)


def _mm_kernel(x_ref, w_ref, o_ref):
    o_ref[...] = jnp.dot(x_ref[...], w_ref[...], preferred_element_type=F32).astype(o_ref.dtype)


def _matmul_cast(x, w, out_dtypes, name):
    m, k = x.shape
    n = w.shape[1]
    bm, bn = ROW_TILE, COL_TILE
    return pl.pallas_call(
        _mm_cast_kernel,
        out_shape=[jax.ShapeDtypeStruct((m, n), dt) for dt in out_dtypes],
        grid=(n // bn, m // bm),
        in_specs=[pl.BlockSpec((bm, k), lambda j, i: (i, 0)),
                  pl.BlockSpec((k, bn), lambda j, i: (0, j))],
        out_specs=[pl.BlockSpec((bm, bn), lambda j, i: (i, j)) for _ in out_dtypes],
        scratch_shapes=[pltpu.VMEM((k, bn), BF16)],
        compiler_params=_params(2),
        name=name,
    )(x, w)


def _matmul(x, w, out_dtype, bn, name):
    m, k = x.shape
    n = w.shape[1]
    bm = ROW_TILE
    return pl.pallas_call(
        _mm_kernel,
        out_shape=jax.ShapeDtypeStruct((m, n), out_dtype),
        grid=(n // bn, m // bm),
        in_specs=[pl.BlockSpec((bm, k), lambda j, i: (i, 0)),
                  pl.BlockSpec((k, bn), lambda j, i: (0, j))],
        out_specs=pl.BlockSpec((bm, bn), lambda j, i: (i, j)),
        compiler_params=_params(2),
        name=name,
    )(x, w)


def _glu_kernel(x_ref, wa_ref, wb_ref, ba_ref, bb_ref, o_ref, wa16_ref, wb16_ref):
    _cast_weights_once([wa_ref, wb_ref], [wa16_ref, wb16_ref], wa16_ref.shape[1])
    x = x_ref[...]
    a = jnp.dot(x, wa16_ref[...], preferred_element_type=F32) + ba_ref[...]
    b = jnp.dot(x, wb16_ref[...], preferred_element_type=F32) + bb_ref[...]
    o_ref[...] = (a * jax.nn.sigmoid(b)).astype(o_ref.dtype)


def _swiglu_kernel(n_valid_last, x_ref, wg_ref, wu_ref, o_ref, wg16_ref, wu16_ref):
    _cast_weights_once([wg_ref, wu_ref], [wg16_ref, wu16_ref], n_valid_last)
    x = x_ref[...]
    g = jnp.dot(x, wg16_ref[...], preferred_element_type=F32)
    u = jnp.dot(x, wu16_ref[...], preferred_element_type=F32)
    o_ref[...] = (g * jax.nn.sigmoid(g) * u).astype(o_ref.dtype)


def _conv_glu(x, w1, b1):
    m, k = x.shape
    n = w1.shape[1] // 2
    bm, bn = ROW_TILE, COL_TILE
    nb = n // bn
    return pl.pallas_call(
        _glu_kernel,
        out_shape=jax.ShapeDtypeStruct((m, n), F32),
        grid=(nb, m // bm),
        in_specs=[pl.BlockSpec((bm, k), lambda j, i: (i, 0)),
                  pl.BlockSpec((k, bn), lambda j, i: (0, j)),
                  pl.BlockSpec((k, bn), lambda j, i: (0, j + nb)),
                  pl.BlockSpec((1, bn), lambda j, i: (0, j)),
                  pl.BlockSpec((1, bn), lambda j, i: (0, j + nb))],
        out_specs=pl.BlockSpec((bm, bn), lambda j, i: (i, j)),
        scratch_shapes=[pltpu.VMEM((k, bn), BF16), pltpu.VMEM((k, bn), BF16)],
        compiler_params=_params(2),
        name="conv_pw1_glu",
    )(x, w1, w1, b1, b1)


def _swiglu(x, wg, wu):
    m, k = x.shape
    bm, bn = ROW_TILE, COL_TILE
    nb = D_FF_PAD // bn
    n_valid_last = wg.shape[1] - (nb - 1) * bn
    return pl.pallas_call(
        functools.partial(_swiglu_kernel, n_valid_last),
        out_shape=jax.ShapeDtypeStruct((m, D_FF_PAD), BF16),
        grid=(nb, m // bm),
        in_specs=[pl.BlockSpec((bm, k), lambda j, i: (i, 0)),
                  pl.BlockSpec((k, bn), lambda j, i: (0, j)),
                  pl.BlockSpec((k, bn), lambda j, i: (0, j))],
        out_specs=pl.BlockSpec((bm, bn), lambda j, i: (i, j)),
        scratch_shapes=[pltpu.VMEM((k, bn), BF16), pltpu.VMEM((k, bn), BF16)],
        compiler_params=_params(2),
        name="ffn_gate_up",
    )(x, wg, wu)


def _layer_norm_rows(x, g, b):
    mu = jnp.mean(x, axis=-1, keepdims=True)
    xc = x - mu
    var = jnp.mean(xc * xc, axis=-1, keepdims=True)
    return xc * lax.rsqrt(var + LN_EPS) * g + b


def _add_ln_rows(r_ref, m_ref, bias_ref, g_ref, b_ref, o_refs):
    def body(c, carry):
        rows = pl.ds(pl.multiple_of(c * LN_ROWS, LN_ROWS), LN_ROWS)
        x = ALPHA * r_ref[rows, :] + (m_ref[rows, :] + bias_ref[...])
        y = _layer_norm_rows(x, g_ref[...], b_ref[...])
        for o_ref in o_refs:
            o_ref[rows, :] = y.astype(o_ref.dtype)
        return carry

    lax.fori_loop(0, r_ref.shape[0] // LN_ROWS, body, 0, unroll=2)


def _add_ln_kernel(n_main, r_main_ref, r_tail_ref, m_ref, bias_ref, g_ref, b_ref, *o_refs):
    i = pl.program_id(0)

    @pl.when(i < n_main)
    def _():
        _add_ln_rows(r_main_ref, m_ref, bias_ref, g_ref, b_ref, o_refs)

    @pl.when(i >= n_main)
    def _():
        _add_ln_rows(r_tail_ref, m_ref, bias_ref, g_ref, b_ref, o_refs)


def _add_ln(r_main, r_tail, m, bias, g, b, out_dtypes, name):
    rows, d = m.shape
    tm = LN_TILE
    n_main = N_PROMPT // tm
    row_spec = pl.BlockSpec((tm, d), lambda i: (i, 0))
    vec_spec = pl.BlockSpec((1, d), lambda i: (0, 0))
    return pl.pallas_call(
        functools.partial(_add_ln_kernel, n_main),
        out_shape=[jax.ShapeDtypeStruct((rows, d), dt) for dt in out_dtypes],
        grid=(rows // tm,),
        in_specs=[pl.BlockSpec((tm, d), lambda i: (jnp.minimum(i, n_main - 1), 0)),
                  pl.BlockSpec((tm, d), lambda i: (jnp.maximum(i - n_main, 0), 0)),
                  row_spec, vec_spec, vec_spec, vec_spec],
        out_specs=[row_spec for _ in out_dtypes],
        compiler_params=_params(1),
        name=name,
    )(r_main, r_tail, m, bias, g, b)


def _final_ln_kernel(n_main, r_ref, m_ref, bias_ref, g_ref, b_ref, o_main_ref, o_sample_ref):
    i = pl.program_id(0)

    @pl.when(i < n_main)
    def _():
        _add_ln_rows(r_ref, m_ref, bias_ref, g_ref, b_ref, [o_main_ref])

    @pl.when(i == n_main)
    def _():
        _add_ln_rows(r_ref, m_ref, bias_ref, g_ref, b_ref, [o_sample_ref])


def _final_ln(r, m, bias, g, b):
    d = m.shape[1]
    tm = LN_TILE
    n_main = N_PROMPT // tm
    row_spec = pl.BlockSpec((tm, d), lambda i: (i, 0))
    vec_spec = pl.BlockSpec((1, d), lambda i: (0, 0))
    return pl.pallas_call(
        functools.partial(_final_ln_kernel, n_main),
        out_shape=[jax.ShapeDtypeStruct((N_PROMPT, d), F32),
                   jax.ShapeDtypeStruct((N_SAMPLE, d), F32)],
        grid=(n_main + 1,),
        in_specs=[row_spec, row_spec, vec_spec, vec_spec, vec_spec],
        out_specs=[pl.BlockSpec((tm, d), lambda i: (jnp.minimum(i, n_main - 1), 0)),
                   pl.BlockSpec((tm, d), lambda i: (0, 0))],
        compiler_params=_params(1),
        name="final_ln",
    )(r, m, bias, g, b)


def _conv_ln_silu(win_ref, wdw_ref, bdw_ref, g_ref, b_ref, o_ref, c_ref, sh_ref, t_rows):
    d = win_ref.shape[1]
    rh = min(CONV_ROWS, t_rows)
    sh_rows = sh_ref.shape[1]

    def col_body(cc, carry):
        cols = pl.ds(pl.multiple_of(cc * CONV_COLS, CONV_COLS), CONV_COLS)
        for r in range(1, SUBLANES):
            sh_ref[r - 1] = win_ref[r:r + sh_rows, cols]
        for r0 in range(0, t_rows, rh):
            acc = jnp.zeros((rh, CONV_COLS), F32) + bdw_ref[:, cols]
            for k in range(CONV_W):
                a, r = divmod(CONV_LEAD + k, SUBLANES)
                lo = r0 + a * SUBLANES
                tap = win_ref[lo:lo + rh, cols] if r == 0 else sh_ref[r - 1, lo:lo + rh, :]
                acc = acc + wdw_ref[k:k + 1, cols] * tap
            c_ref[r0:r0 + rh, cols] = acc
        return carry

    lax.fori_loop(0, d // CONV_COLS, col_body, 0)

    def ln_body(c, carry):
        rows = pl.ds(pl.multiple_of(c * LN_ROWS, LN_ROWS), LN_ROWS)
        y = _layer_norm_rows(c_ref[rows, :], g_ref[...], b_ref[...])
        o_ref[rows, :] = (y * jax.nn.sigmoid(y)).astype(o_ref.dtype)
        return carry

    n_ln = t_rows // LN_ROWS
    lax.fori_loop(0, n_ln, ln_body, 0, unroll=2 if n_ln % 2 == 0 else 1)


def _conv_prompt_kernel(prev_ref, cur_ref, first_ref, wdw_ref, bdw_ref, g_ref, b_ref,
                        o_ref, win_ref, c_ref, sh_ref):
    t_rows = cur_ref.shape[0]
    i = pl.program_id(1)

    @pl.when(i == 0)
    def _():
        win_ref[0:CONV_HALO, :] = first_ref[...]

    @pl.when(i > 0)
    def _():
        win_ref[0:CONV_HALO, :] = prev_ref[...]

    win_ref[CONV_HALO:CONV_HALO + t_rows, :] = cur_ref[...]
    _conv_ln_silu(win_ref, wdw_ref, bdw_ref, g_ref, b_ref, o_ref, c_ref, sh_ref, t_rows)


def _conv_stream_kernel(win_ref, wdw_ref, bdw_ref, g_ref, b_ref, o_ref, c_ref, sh_ref):
    _conv_ln_silu(win_ref, wdw_ref, bdw_ref, g_ref, b_ref, o_ref, c_ref, sh_ref, o_ref.shape[0])


def _conv_scratch(t, d):
    sh_rows = CONV_HALO + t - SUBLANES
    return [pltpu.VMEM((t, d), F32), pltpu.VMEM((SUBLANES - 1, sh_rows, CONV_COLS), F32)]


def _conv_prompt(g, first, wdw, bdw, ln_g, ln_b):
    d = g.shape[1]
    t = CONV_TILE
    tiles = SEQ // t
    halo_per_tile = t // CONV_HALO
    vec = lambda r: pl.BlockSpec((r, d), lambda b, i: (0, 0))
    return pl.pallas_call(
        _conv_prompt_kernel,
        out_shape=jax.ShapeDtypeStruct((N_PROMPT, d), BF16),
        grid=(BATCH, tiles),
        in_specs=[
            pl.BlockSpec((CONV_HALO, d),
                         lambda b, i: (jnp.maximum((b * tiles + i) * halo_per_tile - 1, 0), 0)),
            pl.BlockSpec((t, d), lambda b, i: (b * tiles + i, 0)),
            vec(CONV_HALO), vec(CONV_W), vec(1), vec(1), vec(1)],
        out_specs=pl.BlockSpec((t, d), lambda b, i: (b * tiles + i, 0)),
        scratch_shapes=[pltpu.VMEM((CONV_HALO + t, d), F32)] + _conv_scratch(t, d),
        compiler_params=_params(2),
        name="conv_prompt",
    )(g, g, first, wdw, bdw, ln_g, ln_b)


def _conv_streams(win, wdw, bdw, ln_g, ln_b):
    s, rows, d = win.shape
    t = rows - CONV_HALO
    vec = lambda r: pl.BlockSpec((r, d), lambda i: (0, 0))
    return pl.pallas_call(
        _conv_stream_kernel,
        out_shape=jax.ShapeDtypeStruct((s * t, d), BF16),
        grid=(s,),
        in_specs=[pl.BlockSpec((None, rows, d), lambda i: (i, 0, 0)),
                  vec(CONV_W), vec(1), vec(1), vec(1)],
        out_specs=pl.BlockSpec((t, d), lambda i: (i, 0)),
        scratch_shapes=_conv_scratch(t, d),
        compiler_params=_params(1),
        name="conv_streams",
    )(win, wdw, bdw, ln_g, ln_b)


def _suffix_sum_matrix(n):
    r = lax.broadcasted_iota(jnp.int32, (2 * n, n), 0) % n
    c = lax.broadcasted_iota(jnp.int32, (2 * n, n), 1)
    return jnp.where(r > c, 1.0, 0.0).astype(BF16)


def _bits_op(x, op, bits):
    u = lax.bitcast_convert_type(x, jnp.uint32)
    return lax.bitcast_convert_type(op(u, jnp.uint32(bits)), F32)


def _scores(q, k):
    return lax.dot_general(q, k, (((1,), (1,)), ((), ())), preferred_element_type=F32)


def _sb_terms(s, mask):
    z2 = s * SCORE_SCALE
    neg_abs = _bits_op(z2, jnp.bitwise_or, SIGN_BIT)
    t2 = jnp.log(1.0 + jnp.exp2(neg_abs)) * LOG2E
    log_beta = jnp.minimum(z2, 0.0) - t2
    neg_log_keep = z2 - log_beta
    if mask is not None:
        neg_log_keep = jnp.where(mask, neg_log_keep, 0.0)
    hi = _bits_op(neg_log_keep, jnp.bitwise_and, BF16_BITS)
    lo = neg_log_keep - hi
    hilo = jnp.concatenate([hi.astype(BF16), lo.astype(BF16)], axis=1)
    return log_beta, neg_log_keep[:, 0:1], hilo


def _sb_log_weights(log_beta, first_col, after, carry, mask):
    x = log_beta - after - carry
    if mask is not None:
        x = jnp.where(mask, x, NEG_BIG)
    return x, carry + (after[:, 0:1] + first_col)


def _sb_first_half(scores_fn, upper2, carry, mask):
    log_beta, first_col, hilo = _sb_terms(scores_fn(), mask)
    after = jnp.dot(hilo, upper2, preferred_element_type=F32)
    return _sb_log_weights(log_beta, first_col, after, carry, mask)


def _sb_weighted_values(x, v):
    return jnp.dot(jnp.exp2(x).astype(BF16), v, preferred_element_type=F32)


def _strict_lower_mask(nq, nk, period):
    r = lax.broadcasted_iota(jnp.int32, (nq, nk), 0) % period
    c = lax.broadcasted_iota(jnp.int32, (nq, nk), 1)
    return c < r


def _first_keys_mask(nq, nk, n_valid):
    return lax.broadcasted_iota(jnp.int32, (nq, nk), 1) < n_valid


def _attn_prompt_kernel(q_ref, k_ref, v_ref, km_ref, vm_ref, u_ref, us_ref, o_ref,
                        acc_ref, carry_ref, x_ref):
    qi = pl.program_id(2)
    blk = ATT_BLOCK
    nh = ATT_PROMPT_HEADS
    cols = [slice(h * HEAD_DIM, (h + 1) * HEAD_DIM) for h in range(nh)]

    def block_rows(j):
        return pl.ds(pl.multiple_of(j * blk, blk), blk)

    def step(v_pending, pending_keys, k_next, next_rows, upper2_ref, mask):
        next_keys = upper2_ref.shape[1]
        scores, mid = {}, {}
        for t in range(nh + 2):
            if t < nh:
                scores[t] = _scores(q_ref[:, cols[t]], k_next[next_rows, cols[t]])
            h = t - 1
            if 0 <= h < nh:
                acc_ref[h] += _sb_weighted_values(x_ref[h, :, :pending_keys], v_pending(h))
                log_beta, first_col, hilo = _sb_terms(scores.pop(h), mask)
                mid[h] = (log_beta, first_col,
                          jnp.dot(hilo, upper2_ref[...], preferred_element_type=F32))
            h = t - 2
            if 0 <= h < nh:
                x, carry = _sb_log_weights(*mid.pop(h), carry_ref[h], mask)
                x_ref[h, :, :next_keys] = x
                carry_ref[h] = carry

    diag_mask = _strict_lower_mask(blk, blk, blk)
    for h in range(nh):
        acc_ref[h] = jnp.zeros((blk, HEAD_DIM), F32)
        x, carry = _sb_first_half(
            lambda: _scores(q_ref[:, cols[h]], k_ref[block_rows(qi), cols[h]]),
            u_ref[...], jnp.zeros((blk, 1), F32), diag_mask)
        x_ref[h] = x
        carry_ref[h] = carry

    def body(j, carry):
        step(lambda h: v_ref[block_rows(qi - j), cols[h]], blk,
             k_ref, block_rows(qi - 1 - j), u_ref, None)
        return carry

    lax.fori_loop(0, qi, body, 0)

    step(lambda h: v_ref[block_rows(0), cols[h]], blk, km_ref, slice(None), us_ref,
         _first_keys_mask(blk, SMALL_BLOCK, N_META))
    for h in range(nh):
        out = _sb_weighted_values(x_ref[h, :, :SMALL_BLOCK], vm_ref[:, cols[h]])
        o_ref[:, cols[h]] = (acc_ref[h] + out).astype(o_ref.dtype)


def _attn_prompt(q, kv, upper2, upper2_small):
    blk = ATT_BLOCK
    tiles = SEQ // blk
    hw = ATT_PROMPT_HEADS * HEAD_DIM
    hb = N_HEADS // ATT_PROMPT_HEADS
    meta_blk = META_ROW0 // SMALL_BLOCK
    const = lambda a: pl.BlockSpec(a.shape, lambda b, h, i: (0, 0))
    return pl.pallas_call(
        _attn_prompt_kernel,
        out_shape=jax.ShapeDtypeStruct((N_PROMPT, D_MODEL), BF16),
        grid=(BATCH, hb, tiles),
        in_specs=[
            pl.BlockSpec((blk, hw), lambda b, h, i: (b * tiles + i, h)),
            pl.BlockSpec((SEQ, hw), lambda b, h, i: (b, h)),
            pl.BlockSpec((SEQ, hw), lambda b, h, i: (b, hb + h)),
            pl.BlockSpec((SMALL_BLOCK, hw), lambda b, h, i: (meta_blk, h)),
            pl.BlockSpec((SMALL_BLOCK, hw), lambda b, h, i: (meta_blk, hb + h)),
            const(upper2), const(upper2_small)],
        out_specs=pl.BlockSpec((blk, hw), lambda b, h, i: (b * tiles + i, h)),
        scratch_shapes=[pltpu.VMEM((ATT_PROMPT_HEADS, blk, HEAD_DIM), F32),
                        pltpu.VMEM((ATT_PROMPT_HEADS, blk, 1), F32),
                        pltpu.VMEM((ATT_PROMPT_HEADS, blk, blk), F32)],
        compiler_params=_params(3),
        name="attn_prompt",
    )(q, kv, kv, kv, kv, upper2, upper2_small)


def _head_rows(ref3, start, size, h):
    keys, nh, hd = ref3.shape
    ref2 = ref3.reshape(keys * nh, hd)
    return ref2[pl.ds(start * nh + h, size, stride=nh), :]


def _attn_sample_kernel(q_ref, kn_ref, vn_ref, kc_ref, vc_ref, km_ref, vm_ref, u_ref, us_ref,
                        o_ref):
    blk = ATT_BLOCK
    nh = ATT_SAMPLE_HEADS
    nq = nh * DEC_SEQ
    cols = [slice(h * HEAD_DIM, (h + 1) * HEAD_DIM) for h in range(nh)]
    qrows = [slice(h * DEC_SEQ, (h + 1) * DEC_SEQ) for h in range(nh)]
    fill = jnp.zeros((SMALL_BLOCK - DEC_SEQ, HEAD_DIM), BF16)

    def block(k_of, v_of, u, mask):
        scores = lambda: jnp.concatenate(
            [_scores(q_ref[:, cols[h]], k_of(h)) for h in range(nh)], axis=0)
        return scores, v_of, u, mask

    def cached(ref3, c):
        return lambda h: _head_rows(ref3, c * blk, blk, h).astype(BF16)

    blocks = [block(lambda h: jnp.concatenate([kn_ref[:, cols[h]], fill], axis=0),
                    lambda h: jnp.concatenate([vn_ref[:, cols[h]], fill], axis=0),
                    us_ref, _strict_lower_mask(nq, SMALL_BLOCK, DEC_SEQ))]
    for c in reversed(range(PAST_LEN // blk)):
        blocks.append(block(cached(kc_ref, c), cached(vc_ref, c), u_ref, None))
    blocks.append(block(lambda h: km_ref[:, cols[h]], lambda h: vm_ref[:, cols[h]],
                        us_ref, _first_keys_mask(nq, SMALL_BLOCK, N_META)))

    acc = jnp.zeros((nq, HEAD_DIM), F32)
    pending, carry = _sb_first_half(blocks[0][0], blocks[0][2][...], jnp.zeros((nq, 1), F32),
                                    blocks[0][3])
    for b in range(len(blocks)):
        nxt = None
        if b + 1 < len(blocks):
            nxt, carry = _sb_first_half(blocks[b + 1][0], blocks[b + 1][2][...], carry,
                                        blocks[b + 1][3])
        a = jnp.exp2(pending).astype(BF16)
        acc = acc + jnp.concatenate(
            [jnp.dot(a[qrows[h]], blocks[b][1](h), preferred_element_type=F32)
             for h in range(nh)], axis=0)
        pending = nxt
    for h in range(nh):
        o_ref[:, cols[h]] = acc[qrows[h]].astype(o_ref.dtype)


def _attn_sample(q, kv, cache_k, cache_v, upper2, upper2_small):
    nh = ATT_SAMPLE_HEADS
    hw = nh * HEAD_DIM
    hb = N_HEADS // nh
    row_blk0 = SAMPLE_ROW0 // DEC_SEQ
    meta_blk = META_ROW0 // SMALL_BLOCK
    new = lambda off: pl.BlockSpec((DEC_SEQ, hw), lambda s, h: (row_blk0 + s, off + h))
    meta = lambda off: pl.BlockSpec((SMALL_BLOCK, hw), lambda s, h: (meta_blk, off + h))
    cache = pl.BlockSpec((None, PAST_LEN, nh, HEAD_DIM), lambda s, h: (s, 0, h, 0))
    const = lambda a: pl.BlockSpec(a.shape, lambda s, h: (0, 0))
    return pl.pallas_call(
        _attn_sample_kernel,
        out_shape=jax.ShapeDtypeStruct((N_SAMPLE, D_MODEL), BF16),
        grid=(DEC_BATCH, hb),
        in_specs=[new(0), new(0), new(hb), cache, cache, meta(0), meta(hb),
                  const(upper2), const(upper2_small)],
        out_specs=pl.BlockSpec((DEC_SEQ, hw), lambda s, h: (s, h)),
        compiler_params=_params(2),
        name="attn_sample",
    )(q, kv, kv, cache_k, cache_v, kv, kv, upper2, upper2_small)


def kernel(x_prompt, x_sample, state_conv, cache_k, cache_v, meta, a_w_pw1, a_b_pw1, a_w_dw, a_b_dw, a_ln_g, a_ln_b, a_w_pw2, a_b_pw2, w_kv, w_q, w_o, ln_mix_g, ln_mix_b, ln_ffn_g, ln_ffn_b, w_gate, w_up, w_down):
    d = D_MODEL
    n_pad = N_ROWS - META_ROW0 - N_META
    w_down_b = jnp.pad(w_down.astype(BF16), ((0, 0), (0, D_FF_PAD - D_FF), (0, 0)))
    zero_vec = jnp.zeros((1, d), F32)
    xp = x_prompt.reshape(N_PROMPT, d)
    x_tail = jnp.concatenate([x_sample.reshape(N_SAMPLE, d), meta, jnp.zeros((n_pad, d), F32)],
                             axis=0)
    x0b = jnp.concatenate([xp.astype(BF16), x_tail.astype(BF16)], axis=0)

    def ffn(r_main, r_tail, xb, l):
        h = _swiglu(xb, w_gate[l], w_up[l])
        y = _matmul(h, w_down_b[l], F32, DOWN_COL_TILE, "ffn_down")
        return r_main, r_tail, y, zero_vec, ln_ffn_g[l][None], ln_ffn_b[l][None]

    g = _conv_glu(x0b, a_w_pw1[0], a_b_pw1[0][None])
    g_sample = g[SAMPLE_ROW0:META_ROW0].reshape(DEC_BATCH, DEC_SEQ, d)
    g_meta = g[META_ROW0:META_ROW0 + N_META]
    first = jnp.concatenate([jnp.zeros((CONV_HALO - N_META, d), F32), g_meta], axis=0)
    conv_args = (a_w_dw[0], a_b_dw[0][None], a_ln_g[0][None], a_ln_b[0][None])
    c_prompt = _conv_prompt(g, first, *conv_args)
    hist = jnp.concatenate([state_conv[0], jnp.zeros((1, CONV_W - 1, d), F32)], axis=0)
    new = jnp.concatenate([g_sample, g_meta[None]], axis=0)
    win = jnp.concatenate([jnp.zeros((DEC_BATCH + 1, CONV_LEAD, d), F32), hist, new], axis=1)
    c_tail = _conv_streams(win, *conv_args)
    c = jnp.concatenate([c_prompt, c_tail, jnp.zeros((n_pad, d), BF16)], axis=0)
    (m,) = _matmul_cast(c, a_w_pw2[0], [F32], "conv_pw2")
    x1, x1b = _add_ln(xp, x_tail, m, a_b_pw2[0][None], ln_mix_g[0][None], ln_mix_b[0][None],
                      [F32, BF16], "mix_ln")
    x2, x2b = _add_ln(*ffn(x1, x1[N_PROMPT:], x1b, 0), [F32, BF16], "ffn_ln")

    kv, kvb = _matmul_cast(x2b, w_kv, [F32, BF16], "kv_proj")
    (qb,) = _matmul_cast(x2b, w_q[0], [BF16], "q_proj")
    upper2 = _suffix_sum_matrix(ATT_BLOCK)
    upper2_small = _suffix_sum_matrix(SMALL_BLOCK)
    o_prompt = _attn_prompt(qb, kvb, upper2, upper2_small)
    o_sample = _attn_sample(qb, kvb, cache_k, cache_v, upper2, upper2_small)
    o = jnp.concatenate([o_prompt, o_sample, jnp.zeros((N_ROWS - META_ROW0, d), BF16)], axis=0)
    (m,) = _matmul_cast(o, w_o[0], [F32], "o_proj")
    x3, x3b = _add_ln(x2, x2[N_PROMPT:], m, zero_vec, ln_mix_g[1][None], ln_mix_b[1][None],
                      [F32, BF16], "mix_ln")
    _, _, y, bias, ln_g, ln_b = ffn(x3, None, x3b, 1)
    y_prompt, y_sample = _final_ln(x3, y, bias, ln_g, ln_b)

    y_prompt = y_prompt.reshape(BATCH, SEQ, d)
    y_sample = y_sample.reshape(DEC_BATCH, DEC_SEQ, d)
    g_prompt = g[:N_PROMPT].reshape(BATCH, SEQ, d)
    state_conv_prompt = g_prompt[:, SEQ - (CONV_W - 1):][None]
    state_conv_sample = jnp.concatenate([state_conv[0][:, DEC_SEQ:], g_sample], axis=1)[None]

    def heads(a):
        return a.reshape(a.shape[:-1] + (N_HEADS, HEAD_DIM))

    def prompt_rows(a):
        a_meta = jnp.broadcast_to(a[META_ROW0:META_ROW0 + N_META][None], (BATCH, N_META, d))
        return heads(jnp.concatenate([a_meta, a[:N_PROMPT].reshape(BATCH, SEQ, d)], axis=1))

    def sample_rows(a):
        return heads(a[SAMPLE_ROW0:META_ROW0].reshape(DEC_BATCH, DEC_SEQ, d))

    k_all, v_all = kv[:, :d], kv[:, d:]
    return (y_prompt, y_sample, state_conv_prompt, state_conv_sample,
            prompt_rows(k_all), prompt_rows(v_all), sample_rows(k_all), sample_rows(v_all))
```

```python
import functools

import jax
import jax.numpy as jnp
from jax import lax
from jax.experimental import pallas as pl
from jax.experimental.pallas import tpu as pltpu

D_MODEL = 4096
BATCH = 2
SEQ = 4096
DEC_BATCH = 16
DEC_SEQ = 16
PAST_LEN = 1024
N_META = 16
CONV_W = 31
HEAD_DIM = 128
D_FF = 11008
DEPTH = 2
LN_EPS = 1e-5

N_HEADS = D_MODEL // HEAD_DIM
ALPHA = (2.0 * DEPTH) ** 0.25
SCALE = HEAD_DIM ** -0.5
LOG2E = 1.4426950408889634
SCORE_SCALE = SCALE * LOG2E

LANES = 128
SUBLANES = 8
VMEM_LIMIT = 56 * 1024 * 1024
ROW_TILE = 512
COL_TILE = 512
DOWN_COL_TILE = 512
WIDE_COL_TILE = min(2048, D_MODEL)
CAST_ROWS = 256
LN_TILE = 256
LN_ROWS = 16
CONV_TILE = 256
CONV_ROWS = 64
CONV_COLS = 512
CONV_HALO = 32
ATT_BLOCK = 256
SMALL_BLOCK = 128
ATT_PROMPT_HEADS = 8
ATT_SAMPLE_HEADS = SUBLANES

N_PROMPT = BATCH * SEQ
N_SAMPLE = DEC_BATCH * DEC_SEQ
SAMPLE_ROW0 = N_PROMPT
META_ROW0 = N_PROMPT + N_SAMPLE
N_ROWS = -(-(META_ROW0 + N_META) // ROW_TILE) * ROW_TILE
N_TAIL = N_ROWS - N_PROMPT
D_FF_PAD = -(-D_FF // COL_TILE) * COL_TILE
CONV_LEAD = CONV_HALO - (CONV_W - 1)
SIGN_BIT = 0x80000000
BF16_BITS = 0xFFFF0000
NEG_BIG = -1e30

assert N_PROMPT % ROW_TILE == 0 and N_TAIL % ROW_TILE == 0
assert N_PROMPT % LN_TILE == 0 and N_SAMPLE == LN_TILE
assert META_ROW0 % SMALL_BLOCK == 0 and SEQ % ATT_BLOCK == 0 and PAST_LEN % ATT_BLOCK == 0

F32 = jnp.float32
BF16 = jnp.bfloat16


def _params(n_axes):
    return pltpu.CompilerParams(
        dimension_semantics=("arbitrary",) * n_axes, vmem_limit_bytes=VMEM_LIMIT)


def _cast_weight(w_ref, wb_ref, n_cols):
    k, bn = wb_ref.shape
    if n_cols < bn:
        wb_ref[:, n_cols:] = jnp.zeros((k, bn - n_cols), BF16)

    def body(c, carry):
        rows = pl.ds(pl.multiple_of(c * CAST_ROWS, CAST_ROWS), CAST_ROWS)
        wb_ref[rows, :n_cols] = w_ref[rows, :n_cols].astype(BF16)
        return carry

    lax.fori_loop(0, k // CAST_ROWS, body, 0)


def _cast_weights_once(w_refs, wb_refs, n_valid_last):
    j, i = pl.program_id(0), pl.program_id(1)
    last = pl.num_programs(0) - 1
    bn = wb_refs[0].shape[1]
    if n_valid_last == bn:
        @pl.when(i == 0)
        def _():
            for w_ref, wb_ref in zip(w_refs, wb_refs):
                _cast_weight(w_ref, wb_ref, bn)
    else:
        @pl.when((i == 0) & (j < last))
        def _():
            for w_ref, wb_ref in zip(w_refs, wb_refs):
                _cast_weight(w_ref, wb_ref, bn)

        @pl.when((i == 0) & (j == last))
        def _():
            for w_ref, wb_ref in zip(w_refs, wb_refs):
                _cast_weight(w_ref, wb_ref, n_valid_last)


def _row_specs(x, bm):
    if not isinstance(x, tuple):
        return [x], [pl.BlockSpec((bm, x.shape[1]), lambda j, i: (i, 0))], None
    main, tail = x
    n_main = main.shape[0] // bm
    k = main.shape[1]
    return ([main, tail],
            [pl.BlockSpec((bm, k), lambda j, i: (jnp.minimum(i, n_main - 1), 0)),
             pl.BlockSpec((bm, k), lambda j, i: (jnp.maximum(i - n_main, 0), 0),
                          pipeline_mode=pl.Buffered(1))],
            n_main)


def _n_rows(x):
    return sum(a.shape[0] for a in x) if isinstance(x, tuple) else x.shape[0]


def _on_row_source(n_main, x_refs, body):
    if n_main is None:
        body(x_refs[0])
        return
    i = pl.program_id(1)
    pl.when(i < n_main)(lambda: body(x_refs[0]))
    pl.when(i >= n_main)(lambda: body(x_refs[1]))


def _mm_kernel(n_main, *refs):
    n_x = 1 if n_main is None else 2
    x_refs, (w_ref, *o_refs) = refs[:n_x], refs[n_x:]

    def body(x_ref):
        acc = jnp.dot(x_ref[...], w_ref[...], preferred_element_type=F32)
        for o_ref in o_refs:
            o_ref[...] = acc.astype(o_ref.dtype)

    _on_row_source(n_main, x_refs, body)


def _matmul(x, w, out_dtypes, bn, name, layer=None):
    arrays, x_specs, n_main = _row_specs(x, ROW_TILE)
    m, bm = _n_rows(x), ROW_TILE
    k, n = w.shape[-2:]
    if layer is None:
        w_spec = pl.BlockSpec((k, bn), lambda j, i: (0, j))
    else:
        w_spec = pl.BlockSpec((None, k, bn), lambda j, i: (layer, 0, j))
    return pl.pallas_call(
        functools.partial(_mm_kernel, n_main),
        out_shape=[jax.ShapeDtypeStruct((m, n), dt) for dt in out_dtypes],
        grid=(n // bn, m // bm),
        in_specs=x_specs + [w_spec],
        out_specs=[pl.BlockSpec((bm, bn), lambda j, i: (i, j)) for _ in out_dtypes],
        compiler_params=_params(2),
        name=name,
    )(*arrays, w)


def _glu_kernel(n_main, *refs):
    n_x = 1 if n_main is None else 2
    x_refs, (wa_ref, wb_ref, ba_ref, bb_ref, o_ref, wa16_ref, wb16_ref) = refs[:n_x], refs[n_x:]
    _cast_weights_once([wa_ref, wb_ref], [wa16_ref, wb16_ref], wa16_ref.shape[1])

    def body(x_ref):
        x = x_ref[...]
        a = jnp.dot(x, wa16_ref[...], preferred_element_type=F32) + ba_ref[...]
        b = jnp.dot(x, wb16_ref[...], preferred_element_type=F32) + bb_ref[...]
        o_ref[...] = (a * jax.nn.sigmoid(b)).astype(o_ref.dtype)

    _on_row_source(n_main, x_refs, body)


def _swiglu_kernel(n_valid_last, x_ref, wg_ref, wu_ref, o_ref, wg16_ref, wu16_ref):
    _cast_weights_once([wg_ref, wu_ref], [wg16_ref, wu16_ref], n_valid_last)
    x = x_ref[...]
    g = jnp.dot(x, wg16_ref[...], preferred_element_type=F32)
    u = jnp.dot(x, wu16_ref[...], preferred_element_type=F32)
    o_ref[...] = (g * jax.nn.sigmoid(g) * u).astype(o_ref.dtype)


def _conv_glu(x, w1, b1):
    arrays, x_specs, n_main = _row_specs(x, ROW_TILE)
    m, bm, bn = _n_rows(x), ROW_TILE, COL_TILE
    k = w1.shape[0]
    n = w1.shape[1] // 2
    nb = n // bn
    return pl.pallas_call(
        functools.partial(_glu_kernel, n_main),
        out_shape=jax.ShapeDtypeStruct((m, n), F32),
        grid=(nb, m // bm),
        in_specs=x_specs + [
            pl.BlockSpec((k, bn), lambda j, i: (0, j)),
            pl.BlockSpec((k, bn), lambda j, i: (0, j + nb)),
            pl.BlockSpec((1, bn), lambda j, i: (0, j)),
            pl.BlockSpec((1, bn), lambda j, i: (0, j + nb))],
        out_specs=pl.BlockSpec((bm, bn), lambda j, i: (i, j)),
        scratch_shapes=[pltpu.VMEM((k, bn), BF16), pltpu.VMEM((k, bn), BF16)],
        compiler_params=_params(2),
        name="conv_pw1_glu",
    )(*arrays, w1, w1, b1, b1)


def _swiglu(x, wg, wu, layer):
    m, k = x.shape
    bm, bn = ROW_TILE, COL_TILE
    nb = D_FF_PAD // bn
    n_valid_last = wg.shape[2] - (nb - 1) * bn
    w_spec = pl.BlockSpec((None, k, bn), lambda j, i: (layer, 0, j))
    return pl.pallas_call(
        functools.partial(_swiglu_kernel, n_valid_last),
        out_shape=jax.ShapeDtypeStruct((m, D_FF_PAD), BF16),
        grid=(nb, m // bm),
        in_specs=[pl.BlockSpec((bm, k), lambda j, i: (i, 0)), w_spec, w_spec],
        out_specs=pl.BlockSpec((bm, bn), lambda j, i: (i, j)),
        scratch_shapes=[pltpu.VMEM((k, bn), BF16), pltpu.VMEM((k, bn), BF16)],
        compiler_params=_params(2),
        name="ffn_gate_up",
    )(x, wg, wu)


KV_HEADS = SUBLANES
KV_COLS = KV_HEADS * HEAD_DIM
PROMPT_TILES = N_PROMPT // ROW_TILE
assert PROMPT_TILES % 2 == 0 and SEQ % ROW_TILE == 0 and D_MODEL % KV_COLS == 0


def _store_heads(dst_ref, acc):
    rows, nh, hd = dst_ref.shape
    flat = dst_ref.reshape(rows * nh, hd)
    for h in range(nh):
        flat[pl.ds(h, rows, stride=nh), :] = acc[:, h * hd:(h + 1) * hd]


def _kv_kernel(x_ref, w_ref, kvb_ref, ks_ref, vs_ref, kp_ref, vp_ref,
               stage0_ref, stage1_ref, meta_stage_ref, sem, meta_sem):
    j, i = pl.program_id(0), pl.program_id(1)
    n_kb = D_MODEL // KV_COLS
    tiles_per_stream = SEQ // ROW_TILE
    acc = jnp.dot(x_ref[...], w_ref[...], preferred_element_type=F32)
    kvb_ref[...] = acc.astype(BF16)
    h0 = (j % n_kb) * KV_HEADS

    def prompt_copy(dst_ref, stage_ref, slot, tile):
        b = tile // tiles_per_stream
        r0 = N_META + (tile % tiles_per_stream) * ROW_TILE
        dst = dst_ref.at[b, pl.ds(r0, ROW_TILE), pl.ds(h0, KV_HEADS), :]
        return pltpu.make_async_copy(stage_ref, dst, sem.at[slot])

    def emit(dst_ref, sample_ref):
        for slot, stage_ref in enumerate((stage0_ref, stage1_ref)):
            @pl.when((i < PROMPT_TILES) & (i % 2 == slot))
            def _():
                @pl.when(i >= 2)
                def _():
                    prompt_copy(dst_ref, stage_ref, slot, i - 2).wait()

                _store_heads(stage_ref, acc)
                prompt_copy(dst_ref, stage_ref, slot, i).start()

        @pl.when(i == PROMPT_TILES)
        def _():
            prompt_copy(dst_ref, stage0_ref, 0, PROMPT_TILES - 2).wait()
            prompt_copy(dst_ref, stage1_ref, 1, PROMPT_TILES - 1).wait()
            _store_heads(sample_ref, acc[:N_SAMPLE])
            _store_heads(meta_stage_ref, acc[N_SAMPLE:N_SAMPLE + N_META])
            copies = [pltpu.make_async_copy(
                meta_stage_ref, dst_ref.at[b, pl.ds(0, N_META), pl.ds(h0, KV_HEADS), :],
                meta_sem.at[b]) for b in range(BATCH)]
            for c in copies:
                c.start()
            for c in copies:
                c.wait()

    pl.when(j < n_kb)(lambda: emit(kp_ref, ks_ref))
    pl.when(j >= n_kb)(lambda: emit(vp_ref, vs_ref))


def _kv_proj(x, w_kv):
    m, k = x.shape
    bm, bn = ROW_TILE, KV_COLS
    n_kb = D_MODEL // bn
    assert m // bm == PROMPT_TILES + 1
    small = jax.ShapeDtypeStruct((N_SAMPLE, N_HEADS, HEAD_DIM), F32)
    big = jax.ShapeDtypeStruct((BATCH, N_META + SEQ, N_HEADS, HEAD_DIM), F32)
    return pl.pallas_call(
        _kv_kernel,
        out_shape=[jax.ShapeDtypeStruct((m, 2 * D_MODEL), BF16), small, small, big, big],
        grid=(2 * n_kb, m // bm),
        in_specs=[pl.BlockSpec((bm, k), lambda j, i: (i, 0)),
                  pl.BlockSpec((k, bn), lambda j, i: (0, j))],
        out_specs=[
            pl.BlockSpec((bm, bn), lambda j, i: (i, j)),
            pl.BlockSpec((N_SAMPLE, KV_HEADS, HEAD_DIM),
                         lambda j, i: (0, jnp.minimum(j, n_kb - 1), 0)),
            pl.BlockSpec((N_SAMPLE, KV_HEADS, HEAD_DIM),
                         lambda j, i: (0, jnp.maximum(j - n_kb, 0), 0)),
            pl.BlockSpec(memory_space=pl.ANY),
            pl.BlockSpec(memory_space=pl.ANY)],
        scratch_shapes=[pltpu.VMEM((bm, KV_HEADS, HEAD_DIM), F32),
                        pltpu.VMEM((bm, KV_HEADS, HEAD_DIM), F32),
                        pltpu.VMEM((N_META, KV_HEADS, HEAD_DIM), F32),
                        pltpu.SemaphoreType.DMA((2,)),
                        pltpu.SemaphoreType.DMA((BATCH,))],
        compiler_params=_params(2),
        name="kv_proj",
    )(x, w_kv)


def _layer_norm_rows(x, g, b):
    mu = jnp.mean(x, axis=-1, keepdims=True)
    xc = x - mu
    var = jnp.mean(xc * xc, axis=-1, keepdims=True)
    return xc * lax.rsqrt(var + LN_EPS) * g + b


def _add_ln_rows(r_ref, m_ref, bias_ref, g_ref, b_ref, o_refs):
    def body(c, carry):
        rows = pl.ds(pl.multiple_of(c * LN_ROWS, LN_ROWS), LN_ROWS)
        x = ALPHA * r_ref[rows, :] + (m_ref[rows, :] + bias_ref[...])
        y = _layer_norm_rows(x, g_ref[...], b_ref[...])
        for o_ref in o_refs:
            o_ref[rows, :] = y.astype(o_ref.dtype)
        return carry

    lax.fori_loop(0, r_ref.shape[0] // LN_ROWS, body, 0, unroll=2)


def _add_ln_kernel(n_main, r_main_ref, r_tail_ref, m_ref, bias_ref, g_ref, b_ref, *o_refs):
    i = pl.program_id(0)

    @pl.when(i < n_main)
    def _():
        _add_ln_rows(r_main_ref, m_ref, bias_ref, g_ref, b_ref, o_refs)

    @pl.when(i >= n_main)
    def _():
        _add_ln_rows(r_tail_ref, m_ref, bias_ref, g_ref, b_ref, o_refs)


def _add_ln(r_main, r_tail, m, bias, g, b, out_dtypes, name):
    rows, d = m.shape
    tm = LN_TILE
    n_main = N_PROMPT // tm
    row_spec = pl.BlockSpec((tm, d), lambda i: (i, 0))
    vec_spec = pl.BlockSpec((1, d), lambda i: (0, 0))
    return pl.pallas_call(
        functools.partial(_add_ln_kernel, n_main),
        out_shape=[jax.ShapeDtypeStruct((rows, d), dt) for dt in out_dtypes],
        grid=(rows // tm,),
        in_specs=[pl.BlockSpec((tm, d), lambda i: (jnp.minimum(i, n_main - 1), 0)),
                  pl.BlockSpec((tm, d), lambda i: (jnp.maximum(i - n_main, 0), 0)),
                  row_spec, vec_spec, vec_spec, vec_spec],
        out_specs=[row_spec for _ in out_dtypes],
        compiler_params=_params(1),
        name=name,
    )(r_main, r_tail, m, bias, g, b)


def _final_ln_kernel(n_main, r_ref, m_ref, bias_ref, g_ref, b_ref, o_main_ref, o_sample_ref):
    i = pl.program_id(0)

    @pl.when(i < n_main)
    def _():
        _add_ln_rows(r_ref, m_ref, bias_ref, g_ref, b_ref, [o_main_ref])

    @pl.when(i == n_main)
    def _():
        _add_ln_rows(r_ref, m_ref, bias_ref, g_ref, b_ref, [o_sample_ref])


def _final_ln(r, m, bias, g, b):
    d = m.shape[1]
    tm = LN_TILE
    n_main = N_PROMPT // tm
    row_spec = pl.BlockSpec((tm, d), lambda i: (i, 0))
    vec_spec = pl.BlockSpec((1, d), lambda i: (0, 0))
    return pl.pallas_call(
        functools.partial(_final_ln_kernel, n_main),
        out_shape=[jax.ShapeDtypeStruct((N_PROMPT, d), F32),
                   jax.ShapeDtypeStruct((N_SAMPLE, d), F32)],
        grid=(n_main + 1,),
        in_specs=[row_spec, row_spec, vec_spec, vec_spec, vec_spec],
        out_specs=[pl.BlockSpec((tm, d), lambda i: (jnp.minimum(i, n_main - 1), 0)),
                   pl.BlockSpec((tm, d), lambda i: (0, 0))],
        compiler_params=_params(1),
        name="final_ln",
    )(r, m, bias, g, b)


def _conv_ln_silu(win_ref, wdw_ref, bdw_ref, g_ref, b_ref, o_ref, c_ref, sh_ref, t_rows):
    d = win_ref.shape[1]
    rh = min(CONV_ROWS, t_rows)
    sh_rows = sh_ref.shape[1]

    def col_body(cc, carry):
        cols = pl.ds(pl.multiple_of(cc * CONV_COLS, CONV_COLS), CONV_COLS)
        for r in range(1, SUBLANES):
            sh_ref[r - 1] = win_ref[r:r + sh_rows, cols]
        for r0 in range(0, t_rows, rh):
            acc = jnp.zeros((rh, CONV_COLS), F32) + bdw_ref[:, cols]
            for k in range(CONV_W):
                a, r = divmod(CONV_LEAD + k, SUBLANES)
                lo = r0 + a * SUBLANES
                tap = win_ref[lo:lo + rh, cols] if r == 0 else sh_ref[r - 1, lo:lo + rh, :]
                acc = acc + wdw_ref[k:k + 1, cols] * tap
            c_ref[r0:r0 + rh, cols] = acc
        return carry

    lax.fori_loop(0, d // CONV_COLS, col_body, 0)

    def ln_body(c, carry):
        rows = pl.ds(pl.multiple_of(c * LN_ROWS, LN_ROWS), LN_ROWS)
        y = _layer_norm_rows(c_ref[rows, :], g_ref[...], b_ref[...])
        o_ref[rows, :] = (y * jax.nn.sigmoid(y)).astype(o_ref.dtype)
        return carry

    n_ln = t_rows // LN_ROWS
    lax.fori_loop(0, n_ln, ln_body, 0, unroll=2 if n_ln % 2 == 0 else 1)


def _conv_prompt_kernel(prev_ref, cur_ref, first_ref, wdw_ref, bdw_ref, g_ref, b_ref,
                        o_ref, win_ref, c_ref, sh_ref):
    t_rows = cur_ref.shape[0]
    i = pl.program_id(1)

    @pl.when(i == 0)
    def _():
        win_ref[0:CONV_HALO, :] = first_ref[...]

    @pl.when(i > 0)
    def _():
        win_ref[0:CONV_HALO, :] = prev_ref[...]

    win_ref[CONV_HALO:CONV_HALO + t_rows, :] = cur_ref[...]
    _conv_ln_silu(win_ref, wdw_ref, bdw_ref, g_ref, b_ref, o_ref, c_ref, sh_ref, t_rows)


def _conv_stream_kernel(win_ref, wdw_ref, bdw_ref, g_ref, b_ref, o_ref, c_ref, sh_ref):
    _conv_ln_silu(win_ref, wdw_ref, bdw_ref, g_ref, b_ref, o_ref, c_ref, sh_ref, o_ref.shape[0])


def _conv_scratch(t, d):
    sh_rows = CONV_HALO + t - SUBLANES
    return [pltpu.VMEM((t, d), F32), pltpu.VMEM((SUBLANES - 1, sh_rows, CONV_COLS), F32)]


def _conv_prompt(g, first, wdw, bdw, ln_g, ln_b):
    d = g.shape[1]
    t = CONV_TILE
    tiles = SEQ // t
    halo_per_tile = t // CONV_HALO
    vec = lambda r: pl.BlockSpec((r, d), lambda b, i: (0, 0))
    return pl.pallas_call(
        _conv_prompt_kernel,
        out_shape=jax.ShapeDtypeStruct((N_PROMPT, d), BF16),
        grid=(BATCH, tiles),
        in_specs=[
            pl.BlockSpec((CONV_HALO, d),
                         lambda b, i: (jnp.maximum((b * tiles + i) * halo_per_tile - 1, 0), 0)),
            pl.BlockSpec((t, d), lambda b, i: (b * tiles + i, 0)),
            vec(CONV_HALO), vec(CONV_W), vec(1), vec(1), vec(1)],
        out_specs=pl.BlockSpec((t, d), lambda b, i: (b * tiles + i, 0)),
        scratch_shapes=[pltpu.VMEM((CONV_HALO + t, d), F32)] + _conv_scratch(t, d),
        compiler_params=_params(2),
        name="conv_prompt",
    )(g, g, first, wdw, bdw, ln_g, ln_b)


def _conv_streams(win, wdw, bdw, ln_g, ln_b):
    s, rows, d = win.shape
    t = rows - CONV_HALO
    vec = lambda r: pl.BlockSpec((r, d), lambda i: (0, 0))
    return pl.pallas_call(
        _conv_stream_kernel,
        out_shape=jax.ShapeDtypeStruct((s * t, d), BF16),
        grid=(s,),
        in_specs=[pl.BlockSpec((None, rows, d), lambda i: (i, 0, 0)),
                  vec(CONV_W), vec(1), vec(1), vec(1)],
        out_specs=pl.BlockSpec((t, d), lambda i: (i, 0)),
        scratch_shapes=_conv_scratch(t, d),
        compiler_params=_params(1),
        name="conv_streams",
    )(win, wdw, bdw, ln_g, ln_b)


def _suffix_sum_matrix(n):
    r = lax.broadcasted_iota(jnp.int32, (2 * n, n), 0) % n
    c = lax.broadcasted_iota(jnp.int32, (2 * n, n), 1)
    return jnp.where(r > c, 1.0, 0.0).astype(BF16)


def _bits_op(x, op, bits):
    u = lax.bitcast_convert_type(x, jnp.uint32)
    return lax.bitcast_convert_type(op(u, jnp.uint32(bits)), F32)


def _scores(q, k):
    return lax.dot_general(q, k, (((1,), (1,)), ((), ())), preferred_element_type=F32)


def _sb_terms(s, mask):
    z2 = s * SCORE_SCALE
    neg_abs = _bits_op(z2, jnp.bitwise_or, SIGN_BIT)
    t2 = jnp.log(1.0 + jnp.exp2(neg_abs)) * LOG2E
    log_beta = jnp.minimum(z2, 0.0) - t2
    neg_log_keep = z2 - log_beta
    if mask is not None:
        neg_log_keep = jnp.where(mask, neg_log_keep, 0.0)
    hi = _bits_op(neg_log_keep, jnp.bitwise_and, BF16_BITS)
    lo = neg_log_keep - hi
    hilo = jnp.concatenate([hi.astype(BF16), lo.astype(BF16)], axis=1)
    return log_beta, neg_log_keep[:, 0:1], hilo


def _sb_log_weights(log_beta, first_col, after, carry, mask):
    x = log_beta - after - carry
    if mask is not None:
        x = jnp.where(mask, x, NEG_BIG)
    return x, carry + (after[:, 0:1] + first_col)


def _sb_first_half(scores_fn, upper2, carry, mask):
    log_beta, first_col, hilo = _sb_terms(scores_fn(), mask)
    after = jnp.dot(hilo, upper2, preferred_element_type=F32)
    return _sb_log_weights(log_beta, first_col, after, carry, mask)


def _sb_weighted_values(x, v):
    return jnp.dot(jnp.exp2(x).astype(BF16), v, preferred_element_type=F32)


def _strict_lower_mask(nq, nk, period):
    r = lax.broadcasted_iota(jnp.int32, (nq, nk), 0) % period
    c = lax.broadcasted_iota(jnp.int32, (nq, nk), 1)
    return c < r


def _first_keys_mask(nq, nk, n_valid):
    return lax.broadcasted_iota(jnp.int32, (nq, nk), 1) < n_valid


def _attn_prompt_kernel(q_ref, k_ref, v_ref, km_ref, vm_ref, u_ref, us_ref, o_ref,
                        acc_ref, carry_ref, x_ref):
    qi = pl.program_id(2)
    blk = ATT_BLOCK
    nh = ATT_PROMPT_HEADS
    cols = [slice(h * HEAD_DIM, (h + 1) * HEAD_DIM) for h in range(nh)]

    def block_rows(j):
        return pl.ds(pl.multiple_of(j * blk, blk), blk)

    def step(v_pending, pending_keys, k_next, next_rows, upper2_ref, mask):
        next_keys = upper2_ref.shape[1]
        scores, mid = {}, {}
        for t in range(nh + 2):
            if t < nh:
                scores[t] = _scores(q_ref[:, cols[t]], k_next[next_rows, cols[t]])
            h = t - 1
            if 0 <= h < nh:
                acc_ref[h] += _sb_weighted_values(x_ref[h, :, :pending_keys], v_pending(h))
                log_beta, first_col, hilo = _sb_terms(scores.pop(h), mask)
                mid[h] = (log_beta, first_col,
                          jnp.dot(hilo, upper2_ref[...], preferred_element_type=F32))
            h = t - 2
            if 0 <= h < nh:
                x, carry = _sb_log_weights(*mid.pop(h), carry_ref[h], mask)
                x_ref[h, :, :next_keys] = x
                carry_ref[h] = carry

    diag_mask = _strict_lower_mask(blk, blk, blk)
    for h in range(nh):
        acc_ref[h] = jnp.zeros((blk, HEAD_DIM), F32)
        x, carry = _sb_first_half(
            lambda: _scores(q_ref[:, cols[h]], k_ref[block_rows(qi), cols[h]]),
            u_ref[...], jnp.zeros((blk, 1), F32), diag_mask)
        x_ref[h] = x
        carry_ref[h] = carry

    def body(j, carry):
        step(lambda h: v_ref[block_rows(qi - j), cols[h]], blk,
             k_ref, block_rows(qi - 1 - j), u_ref, None)
        return carry

    lax.fori_loop(0, qi, body, 0)

    step(lambda h: v_ref[block_rows(0), cols[h]], blk, km_ref, slice(None), us_ref,
         _first_keys_mask(blk, SMALL_BLOCK, N_META))
    for h in range(nh):
        out = _sb_weighted_values(x_ref[h, :, :SMALL_BLOCK], vm_ref[:, cols[h]])
        o_ref[:, cols[h]] = (acc_ref[h] + out).astype(o_ref.dtype)


def _attn_prompt(q, kv, upper2, upper2_small):
    blk = ATT_BLOCK
    tiles = SEQ // blk
    hw = ATT_PROMPT_HEADS * HEAD_DIM
    hb = N_HEADS // ATT_PROMPT_HEADS
    meta_blk = META_ROW0 // SMALL_BLOCK
    const = lambda a: pl.BlockSpec(a.shape, lambda b, h, i: (0, 0))
    return pl.pallas_call(
        _attn_prompt_kernel,
        out_shape=jax.ShapeDtypeStruct((N_PROMPT, D_MODEL), BF16),
        grid=(BATCH, hb, tiles),
        in_specs=[
            pl.BlockSpec((blk, hw), lambda b, h, i: (b * tiles + i, h)),
            pl.BlockSpec((SEQ, hw), lambda b, h, i: (b, h)),
            pl.BlockSpec((SEQ, hw), lambda b, h, i: (b, hb + h)),
            pl.BlockSpec((SMALL_BLOCK, hw), lambda b, h, i: (meta_blk, h)),
            pl.BlockSpec((SMALL_BLOCK, hw), lambda b, h, i: (meta_blk, hb + h)),
            const(upper2), const(upper2_small)],
        out_specs=pl.BlockSpec((blk, hw), lambda b, h, i: (b * tiles + i, h)),
        scratch_shapes=[pltpu.VMEM((ATT_PROMPT_HEADS, blk, HEAD_DIM), F32),
                        pltpu.VMEM((ATT_PROMPT_HEADS, blk, 1), F32),
                        pltpu.VMEM((ATT_PROMPT_HEADS, blk, blk), F32)],
        compiler_params=_params(3),
        name="attn_prompt",
    )(q, kv, kv, kv, kv, upper2, upper2_small)


def _head_rows(ref3, start, size, h):
    keys, nh, hd = ref3.shape
    ref2 = ref3.reshape(keys * nh, hd)
    return ref2[pl.ds(start * nh + h, size, stride=nh), :]


def _attn_sample_kernel(q_ref, kn_ref, vn_ref, kc_ref, vc_ref, km_ref, vm_ref, u_ref, us_ref,
                        o_ref):
    blk = ATT_BLOCK
    nh = ATT_SAMPLE_HEADS
    nq = nh * DEC_SEQ
    cols = [slice(h * HEAD_DIM, (h + 1) * HEAD_DIM) for h in range(nh)]
    qrows = [slice(h * DEC_SEQ, (h + 1) * DEC_SEQ) for h in range(nh)]
    fill = jnp.zeros((SMALL_BLOCK - DEC_SEQ, HEAD_DIM), BF16)

    def block(k_of, v_of, u, mask):
        scores = lambda: jnp.concatenate(
            [_scores(q_ref[:, cols[h]], k_of(h)) for h in range(nh)], axis=0)
        return scores, v_of, u, mask

    def cached(ref3, c):
        return lambda h: _head_rows(ref3, c * blk, blk, h).astype(BF16)

    blocks = [block(lambda h: jnp.concatenate([kn_ref[:, cols[h]], fill], axis=0),
                    lambda h: jnp.concatenate([vn_ref[:, cols[h]], fill], axis=0),
                    us_ref, _strict_lower_mask(nq, SMALL_BLOCK, DEC_SEQ))]
    for c in reversed(range(PAST_LEN // blk)):
        blocks.append(block(cached(kc_ref, c), cached(vc_ref, c), u_ref, None))
    blocks.append(block(lambda h: km_ref[:, cols[h]], lambda h: vm_ref[:, cols[h]],
                        us_ref, _first_keys_mask(nq, SMALL_BLOCK, N_META)))

    acc = jnp.zeros((nq, HEAD_DIM), F32)
    pending, carry = _sb_first_half(blocks[0][0], blocks[0][2][...], jnp.zeros((nq, 1), F32),
                                    blocks[0][3])
    for b in range(len(blocks)):
        nxt = None
        if b + 1 < len(blocks):
            nxt, carry = _sb_first_half(blocks[b + 1][0], blocks[b + 1][2][...], carry,
                                        blocks[b + 1][3])
        a = jnp.exp2(pending).astype(BF16)
        acc = acc + jnp.concatenate(
            [jnp.dot(a[qrows[h]], blocks[b][1](h), preferred_element_type=F32)
             for h in range(nh)], axis=0)
        pending = nxt
    for h in range(nh):
        o_ref[:, cols[h]] = acc[qrows[h]].astype(o_ref.dtype)


def _attn_sample(q, kv, cache_k, cache_v, upper2, upper2_small):
    nh = ATT_SAMPLE_HEADS
    hw = nh * HEAD_DIM
    hb = N_HEADS // nh
    row_blk0 = SAMPLE_ROW0 // DEC_SEQ
    meta_blk = META_ROW0 // SMALL_BLOCK
    new = lambda off: pl.BlockSpec((DEC_SEQ, hw), lambda s, h: (row_blk0 + s, off + h))
    meta = lambda off: pl.BlockSpec((SMALL_BLOCK, hw), lambda s, h: (meta_blk, off + h))
    cache = pl.BlockSpec((None, PAST_LEN, nh, HEAD_DIM), lambda s, h: (s, 0, h, 0))
    const = lambda a: pl.BlockSpec(a.shape, lambda s, h: (0, 0))
    return pl.pallas_call(
        _attn_sample_kernel,
        out_shape=jax.ShapeDtypeStruct((N_SAMPLE, D_MODEL), BF16),
        grid=(DEC_BATCH, hb),
        in_specs=[new(0), new(0), new(hb), cache, cache, meta(0), meta(hb),
                  const(upper2), const(upper2_small)],
        out_specs=pl.BlockSpec((DEC_SEQ, hw), lambda s, h: (s, h)),
        compiler_params=_params(2),
        name="attn_sample",
    )(q, kv, kv, cache_k, cache_v, kv, kv, upper2, upper2_small)


def kernel(x_prompt, x_sample, state_conv, cache_k, cache_v, meta, a_w_pw1, a_b_pw1, a_w_dw, a_b_dw, a_ln_g, a_ln_b, a_w_pw2, a_b_pw2, w_kv, w_q, w_o, ln_mix_g, ln_mix_b, ln_ffn_g, ln_ffn_b, w_gate, w_up, w_down):
    d = D_MODEL
    n_pad = N_ROWS - META_ROW0 - N_META
    w_down_b = jnp.pad(w_down.astype(BF16), ((0, 0), (0, D_FF_PAD - D_FF), (0, 0)))
    zero_vec = jnp.zeros((1, d), F32)
    xp = x_prompt.reshape(N_PROMPT, d)
    x_tail = jnp.concatenate([x_sample.reshape(N_SAMPLE, d), meta, jnp.zeros((n_pad, d), F32)],
                             axis=0)
    tail_pad = jnp.zeros((N_TAIL - N_SAMPLE, d), BF16)

    def ffn(r_main, r_tail, xb, l):
        h = _swiglu(xb, w_gate, w_up, l)
        (y,) = _matmul(h, w_down_b, [F32], DOWN_COL_TILE, "ffn_down", layer=l)
        return r_main, r_tail, y, zero_vec, ln_ffn_g[l][None], ln_ffn_b[l][None]

    x0b = jnp.concatenate([xp.astype(BF16), x_tail.astype(BF16)], axis=0)
    g = _conv_glu(x0b, a_w_pw1[0], a_b_pw1[0][None])
    g_sample = g[SAMPLE_ROW0:META_ROW0].reshape(DEC_BATCH, DEC_SEQ, d)
    g_meta = g[META_ROW0:META_ROW0 + N_META]
    first = jnp.concatenate([jnp.zeros((CONV_HALO - N_META, d), F32), g_meta], axis=0)
    conv_args = (a_w_dw[0], a_b_dw[0][None], a_ln_g[0][None], a_ln_b[0][None])
    c_prompt = _conv_prompt(g, first, *conv_args)
    hist = jnp.concatenate([state_conv[0], jnp.zeros((1, CONV_W - 1, d), F32)], axis=0)
    new = jnp.concatenate([g_sample, g_meta[None]], axis=0)
    win = jnp.concatenate([jnp.zeros((DEC_BATCH + 1, CONV_LEAD, d), F32), hist, new], axis=1)
    c_tail = jnp.concatenate([_conv_streams(win, *conv_args), jnp.zeros((n_pad, d), BF16)],
                             axis=0)
    (m,) = _matmul((c_prompt, c_tail), a_w_pw2[0].astype(BF16), [F32], WIDE_COL_TILE,
                   "conv_pw2")
    x1, x1b = _add_ln(xp, x_tail, m, a_b_pw2[0][None], ln_mix_g[0][None], ln_mix_b[0][None],
                      [F32, BF16], "mix_ln")
    x2, x2b = _add_ln(*ffn(x1, x1[N_PROMPT:], x1b, 0), [F32, BF16], "ffn_ln")

    kvb, k_s, v_s, k_p, v_p = _kv_proj(x2b, w_kv.astype(BF16))
    (qb,) = _matmul(x2b, w_q[0].astype(BF16), [BF16], WIDE_COL_TILE, "q_proj")
    upper2 = _suffix_sum_matrix(ATT_BLOCK)
    upper2_small = _suffix_sum_matrix(SMALL_BLOCK)
    o_prompt = _attn_prompt(qb, kvb, upper2, upper2_small)
    o_sample = _attn_sample(qb, kvb, cache_k, cache_v, upper2, upper2_small)
    o_tail = jnp.concatenate([o_sample, tail_pad], axis=0)
    (m,) = _matmul((o_prompt, o_tail), w_o[0].astype(BF16), [F32], WIDE_COL_TILE, "o_proj")
    x3, x3b = _add_ln(x2, x2[N_PROMPT:], m, zero_vec, ln_mix_g[1][None], ln_mix_b[1][None],
                      [F32, BF16], "mix_ln")
    _, _, y, bias, ln_g, ln_b = ffn(x3, None, x3b, 1)
    y_prompt, y_sample = _final_ln(x3, y, bias, ln_g, ln_b)

    y_prompt = y_prompt.reshape(BATCH, SEQ, d)
    y_sample = y_sample.reshape(DEC_BATCH, DEC_SEQ, d)
    n_ctx = CONV_W - 1
    state_conv_prompt = jnp.stack(
        [g[(b + 1) * SEQ - n_ctx:(b + 1) * SEQ] for b in range(BATCH)], axis=0)[None]
    state_conv_sample = jnp.concatenate([state_conv[0][:, DEC_SEQ:], g_sample], axis=1)[None]

    sample_shape = (DEC_BATCH, DEC_SEQ, N_HEADS, HEAD_DIM)
    return (y_prompt, y_sample, state_conv_prompt, state_conv_sample,
            k_p, v_p, k_s.reshape(sample_shape), v_s.reshape(sample_shape))
```

```python
import functools

import jax
import jax.numpy as jnp
from jax import lax
from jax.experimental import pallas as pl
from jax.experimental.pallas import tpu as pltpu

D_MODEL = 4096
BATCH = 2
SEQ = 4096
DEC_BATCH = 16
DEC_SEQ = 16
PAST_LEN = 1024
N_META = 16
CONV_W = 31
HEAD_DIM = 128
D_FF = 11008
DEPTH = 2
LN_EPS = 1e-5

N_HEADS = D_MODEL // HEAD_DIM
ALPHA = (2.0 * DEPTH) ** 0.25
SCALE = HEAD_DIM ** -0.5
LOG2E = 1.4426950408889634
SCORE_SCALE = SCALE * LOG2E

LANES = 128
SUBLANES = 8
VMEM_LIMIT = 56 * 1024 * 1024
VMEM_LIMIT_PW1 = 60 * 1024 * 1024
ROW_TILE = 512
COL_TILE = 512
DOWN_COL_TILE = 512
GU_COL_TILE = 1024
GU_CHUNK = 256
WIDE_COL_TILE = min(2048, D_MODEL)
CAST_ROWS = 256
LN_TILE = 256
LN_ROWS = 16
CONV_TILE = 256
CONV_ROWS = 64
CONV_COLS = 512
CONV_HALO = 32
ATT_BLOCK = 256
SMALL_BLOCK = 128
ATT_PROMPT_HEADS = 8
ATT_SAMPLE_HEADS = SUBLANES

N_PROMPT = BATCH * SEQ
N_SAMPLE = DEC_BATCH * DEC_SEQ
SAMPLE_ROW0 = N_PROMPT
META_ROW0 = N_PROMPT + N_SAMPLE
N_ROWS = -(-(META_ROW0 + N_META) // ROW_TILE) * ROW_TILE
N_TAIL = N_ROWS - N_PROMPT
CONV_LEAD = CONV_HALO - (CONV_W - 1)
SIGN_BIT = 0x80000000
BF16_BITS = 0xFFFF0000
NEG_BIG = -1e30

assert N_PROMPT % ROW_TILE == 0 and N_TAIL % ROW_TILE == 0
assert N_PROMPT % LN_TILE == 0 and N_SAMPLE == LN_TILE
assert META_ROW0 % SMALL_BLOCK == 0 and SEQ % ATT_BLOCK == 0 and PAST_LEN % ATT_BLOCK == 0

F32 = jnp.float32
BF16 = jnp.bfloat16


def _params(n_axes, vmem_limit=VMEM_LIMIT):
    return pltpu.CompilerParams(
        dimension_semantics=("arbitrary",) * n_axes, vmem_limit_bytes=vmem_limit)


def _cast_weight(w_ref, wb_ref, n_cols):
    k, bn = wb_ref.shape
    if n_cols < bn:
        wb_ref[:, n_cols:] = jnp.zeros((k, bn - n_cols), BF16)

    def body(c, carry):
        rows = pl.ds(pl.multiple_of(c * CAST_ROWS, CAST_ROWS), CAST_ROWS)
        wb_ref[rows, :n_cols] = w_ref[rows, :n_cols].astype(BF16)
        return carry

    lax.fori_loop(0, k // CAST_ROWS, body, 0)


def _cast_weights_once(w_refs, wb_refs, n_valid_last):
    j, i = pl.program_id(0), pl.program_id(1)
    last = pl.num_programs(0) - 1
    bn = wb_refs[0].shape[1]
    if n_valid_last == bn:
        @pl.when(i == 0)
        def _():
            for w_ref, wb_ref in zip(w_refs, wb_refs):
                _cast_weight(w_ref, wb_ref, bn)
    else:
        @pl.when((i == 0) & (j < last))
        def _():
            for w_ref, wb_ref in zip(w_refs, wb_refs):
                _cast_weight(w_ref, wb_ref, bn)

        @pl.when((i == 0) & (j == last))
        def _():
            for w_ref, wb_ref in zip(w_refs, wb_refs):
                _cast_weight(w_ref, wb_ref, n_valid_last)


def _row_specs(x, bm):
    if not isinstance(x, tuple):
        return [x], [pl.BlockSpec((bm, x.shape[1]), lambda j, i: (i, 0))], None
    main, tail = x
    n_main = main.shape[0] // bm
    k = main.shape[1]
    return ([main, tail],
            [pl.BlockSpec((bm, k), lambda j, i: (jnp.minimum(i, n_main - 1), 0)),
             pl.BlockSpec((bm, k), lambda j, i: (jnp.maximum(i - n_main, 0), 0),
                          pipeline_mode=pl.Buffered(1))],
            n_main)


def _n_rows(x):
    return sum(a.shape[0] for a in x) if isinstance(x, tuple) else x.shape[0]


def _on_row_source(n_main, x_refs, body):
    if n_main is None:
        body(x_refs[0])
        return
    i = pl.program_id(1)
    pl.when(i < n_main)(lambda: body(x_refs[0]))
    pl.when(i >= n_main)(lambda: body(x_refs[1]))


def _mm_kernel(n_main, scale, *refs):
    n_x = 1 if n_main is None else 2
    x_refs, (w_ref, *o_refs) = refs[:n_x], refs[n_x:]

    def body(x_ref):
        acc = jnp.dot(x_ref[...], w_ref[...], preferred_element_type=F32)
        if scale is not None:
            acc = acc * scale
        for o_ref in o_refs:
            o_ref[...] = acc.astype(o_ref.dtype)

    _on_row_source(n_main, x_refs, body)


def _matmul(x, w, out_dtypes, bn, name, layer=None, scale=None):
    arrays, x_specs, n_main = _row_specs(x, ROW_TILE)
    m, bm = _n_rows(x), ROW_TILE
    k, n = w.shape[-2:]
    if layer is None:
        w_spec = pl.BlockSpec((k, bn), lambda j, i: (0, j))
    else:
        w_spec = pl.BlockSpec((None, k, bn), lambda j, i: (layer, 0, j))
    return pl.pallas_call(
        functools.partial(_mm_kernel, n_main, scale),
        out_shape=[jax.ShapeDtypeStruct((m, n), dt) for dt in out_dtypes],
        grid=(n // bn, m // bm),
        in_specs=x_specs + [w_spec],
        out_specs=[pl.BlockSpec((bm, bn), lambda j, i: (i, j)) for _ in out_dtypes],
        compiler_params=_params(2),
        name=name,
    )(*arrays, w)


def _glu_kernel(n_main, *refs):
    n_x = 1 if n_main is None else 2
    x_refs, (wa_ref, wb_ref, ba_ref, bb_ref, o_ref, wa16_ref, wb16_ref) = refs[:n_x], refs[n_x:]
    _cast_weights_once([wa_ref, wb_ref], [wa16_ref, wb16_ref], wa16_ref.shape[1])

    def body(x_ref):
        x = x_ref[...]
        a = jnp.dot(x, wa16_ref[...], preferred_element_type=F32) + ba_ref[...]
        b = jnp.dot(x, wb16_ref[...], preferred_element_type=F32) + bb_ref[...]
        o_ref[...] = (a * jax.nn.sigmoid(b)).astype(o_ref.dtype)

    _on_row_source(n_main, x_refs, body)


def _swiglu_kernel(layer, n_tiles, n_valid_last, x_ref, wg_hbm, wu_hbm, o_ref,
                   w16_ref, stage_ref, sem):
    j, i = pl.program_id(0), pl.program_id(1)
    k, bn = w16_ref.shape[2:]
    n_chunks = k // GU_CHUNK
    hbm = (wg_hbm, wu_hbm)

    def chunk_copy(tile, c, w, n_cols):
        src = hbm[w].at[layer, pl.ds(c * GU_CHUNK, GU_CHUNK), pl.ds(tile * bn, n_cols)]
        return pltpu.make_async_copy(src, stage_ref.at[c % 2, w, :, pl.ds(0, n_cols)],
                                     sem.at[c % 2, w])

    def fetch_chunk(tile, c, n_cols):
        slot = tile % 2

        @pl.when(c == 0)
        def _():
            for w in range(2):
                if n_cols < bn:
                    w16_ref[slot, w, :, n_cols:] = jnp.zeros((k, bn - n_cols), BF16)
                chunk_copy(tile, 0, w, n_cols).start()

        rows = pl.ds(pl.multiple_of(c * GU_CHUNK, GU_CHUNK), GU_CHUNK)
        for w in range(2):
            chunk_copy(tile, c, w, n_cols).wait()
            w16_ref[slot, w, rows, :n_cols] = stage_ref[c % 2, w, :, :n_cols].astype(BF16)

        @pl.when(c + 1 < n_chunks)
        def _():
            for w in range(2):
                chunk_copy(tile, c + 1, w, n_cols).start()

    def fetch(tile, c):
        if n_valid_last == bn:
            fetch_chunk(tile, c, bn)
        else:
            pl.when(tile < n_tiles - 1)(lambda: fetch_chunk(tile, c, bn))
            pl.when(tile == n_tiles - 1)(lambda: fetch_chunk(tile, c, n_valid_last))

    @pl.when((j == 0) & (i == 0))
    def _():
        def body(c, carry):
            fetch(j, c)
            return carry

        lax.fori_loop(0, n_chunks, body, 0)

    pl.when((j + 1 < n_tiles) & (i < n_chunks))(lambda: fetch(j + 1, i))

    x = x_ref[...]
    g = jnp.dot(x, w16_ref[j % 2, 0], preferred_element_type=F32)
    u = jnp.dot(x, w16_ref[j % 2, 1], preferred_element_type=F32)
    o_ref[...] = (g * jax.nn.sigmoid(g) * u).astype(o_ref.dtype)


def _conv_glu(x, w1, b1):
    arrays, x_specs, n_main = _row_specs(x, ROW_TILE)
    m, bm, bn = _n_rows(x), ROW_TILE, COL_TILE
    k = w1.shape[0]
    n = w1.shape[1] // 2
    nb = n // bn
    return pl.pallas_call(
        functools.partial(_glu_kernel, n_main),
        out_shape=jax.ShapeDtypeStruct((m, n), F32),
        grid=(nb, m // bm),
        in_specs=x_specs + [
            pl.BlockSpec((k, bn), lambda j, i: (0, j)),
            pl.BlockSpec((k, bn), lambda j, i: (0, j + nb)),
            pl.BlockSpec((1, bn), lambda j, i: (0, j)),
            pl.BlockSpec((1, bn), lambda j, i: (0, j + nb))],
        out_specs=pl.BlockSpec((bm, bn), lambda j, i: (i, j)),
        scratch_shapes=[pltpu.VMEM((k, bn), BF16), pltpu.VMEM((k, bn), BF16)],
        compiler_params=_params(2, VMEM_LIMIT_PW1),
        name="conv_pw1_glu",
    )(*arrays, w1, w1, b1, b1)


def _swiglu(x, wg, wu, layer):
    m, k = x.shape
    bm, bn = ROW_TILE, GU_COL_TILE
    nb = pl.cdiv(D_FF, bn)
    n_valid_last = D_FF - (nb - 1) * bn
    assert k % GU_CHUNK == 0 and k // GU_CHUNK <= m // bm
    hbm_spec = pl.BlockSpec(memory_space=pl.ANY)
    return pl.pallas_call(
        functools.partial(_swiglu_kernel, layer, nb, n_valid_last),
        out_shape=jax.ShapeDtypeStruct((m, D_FF), BF16),
        grid=(nb, m // bm),
        in_specs=[pl.BlockSpec((bm, k), lambda j, i: (i, 0)), hbm_spec, hbm_spec],
        out_specs=pl.BlockSpec((bm, bn), lambda j, i: (i, j)),
        scratch_shapes=[pltpu.VMEM((2, 2, k, bn), BF16),
                        pltpu.VMEM((2, 2, GU_CHUNK, bn), F32),
                        pltpu.SemaphoreType.DMA((2, 2))],
        compiler_params=_params(2),
        name="ffn_gate_up",
    )(x, wg, wu)


KV_HEADS = SUBLANES
KV_COLS = KV_HEADS * HEAD_DIM
PROMPT_TILES = N_PROMPT // ROW_TILE
assert PROMPT_TILES % 2 == 0 and SEQ % ROW_TILE == 0 and D_MODEL % KV_COLS == 0


def _store_heads(dst_ref, acc):
    rows, nh, hd = dst_ref.shape
    flat = dst_ref.reshape(rows * nh, hd)
    for h in range(nh):
        flat[pl.ds(h, rows, stride=nh), :] = acc[:, h * hd:(h + 1) * hd]


def _kv_kernel(x_ref, w_ref, kvb_ref, ks_ref, vs_ref, kp_ref, vp_ref,
               stage0_ref, stage1_ref, meta_stage_ref, sem, meta_sem):
    j, i = pl.program_id(0), pl.program_id(1)
    n_kb = D_MODEL // KV_COLS
    tiles_per_stream = SEQ // ROW_TILE
    acc = jnp.dot(x_ref[...], w_ref[...], preferred_element_type=F32)
    kvb_ref[...] = acc.astype(BF16)
    h0 = (j % n_kb) * KV_HEADS

    def prompt_copy(dst_ref, stage_ref, slot, tile):
        b = tile // tiles_per_stream
        r0 = N_META + (tile % tiles_per_stream) * ROW_TILE
        dst = dst_ref.at[b, pl.ds(r0, ROW_TILE), pl.ds(h0, KV_HEADS), :]
        return pltpu.make_async_copy(stage_ref, dst, sem.at[slot])

    def emit(dst_ref, sample_ref):
        for slot, stage_ref in enumerate((stage0_ref, stage1_ref)):
            @pl.when((i < PROMPT_TILES) & (i % 2 == slot))
            def _():
                @pl.when(i >= 2)
                def _():
                    prompt_copy(dst_ref, stage_ref, slot, i - 2).wait()

                _store_heads(stage_ref, acc)
                prompt_copy(dst_ref, stage_ref, slot, i).start()

        @pl.when(i == PROMPT_TILES)
        def _():
            prompt_copy(dst_ref, stage0_ref, 0, PROMPT_TILES - 2).wait()
            prompt_copy(dst_ref, stage1_ref, 1, PROMPT_TILES - 1).wait()
            _store_heads(sample_ref, acc[:N_SAMPLE])
            _store_heads(meta_stage_ref, acc[N_SAMPLE:N_SAMPLE + N_META])
            copies = [pltpu.make_async_copy(
                meta_stage_ref, dst_ref.at[b, pl.ds(0, N_META), pl.ds(h0, KV_HEADS), :],
                meta_sem.at[b]) for b in range(BATCH)]
            for c in copies:
                c.start()
            for c in copies:
                c.wait()

    pl.when(j < n_kb)(lambda: emit(kp_ref, ks_ref))
    pl.when(j >= n_kb)(lambda: emit(vp_ref, vs_ref))


def _kv_proj(x, w_kv):
    m, k = x.shape
    bm, bn = ROW_TILE, KV_COLS
    n_kb = D_MODEL // bn
    assert m // bm == PROMPT_TILES + 1
    small = jax.ShapeDtypeStruct((N_SAMPLE, N_HEADS, HEAD_DIM), F32)
    big = jax.ShapeDtypeStruct((BATCH, N_META + SEQ, N_HEADS, HEAD_DIM), F32)
    return pl.pallas_call(
        _kv_kernel,
        out_shape=[jax.ShapeDtypeStruct((m, 2 * D_MODEL), BF16), small, small, big, big],
        grid=(2 * n_kb, m // bm),
        in_specs=[pl.BlockSpec((bm, k), lambda j, i: (i, 0)),
                  pl.BlockSpec((k, bn), lambda j, i: (0, j))],
        out_specs=[
            pl.BlockSpec((bm, bn), lambda j, i: (i, j)),
            pl.BlockSpec((N_SAMPLE, KV_HEADS, HEAD_DIM),
                         lambda j, i: (0, jnp.minimum(j, n_kb - 1), 0)),
            pl.BlockSpec((N_SAMPLE, KV_HEADS, HEAD_DIM),
                         lambda j, i: (0, jnp.maximum(j - n_kb, 0), 0)),
            pl.BlockSpec(memory_space=pl.ANY),
            pl.BlockSpec(memory_space=pl.ANY)],
        scratch_shapes=[pltpu.VMEM((bm, KV_HEADS, HEAD_DIM), F32),
                        pltpu.VMEM((bm, KV_HEADS, HEAD_DIM), F32),
                        pltpu.VMEM((N_META, KV_HEADS, HEAD_DIM), F32),
                        pltpu.SemaphoreType.DMA((2,)),
                        pltpu.SemaphoreType.DMA((BATCH,))],
        compiler_params=_params(2),
        name="kv_proj",
    )(x, w_kv)


def _layer_norm_rows(x, g, b):
    mu = jnp.mean(x, axis=-1, keepdims=True)
    xc = x - mu
    var = jnp.mean(xc * xc, axis=-1, keepdims=True)
    return xc * lax.rsqrt(var + LN_EPS) * g + b


def _add_ln_rows(r_ref, m_ref, bias_ref, g_ref, b_ref, o_refs):
    def body(c, carry):
        rows = pl.ds(pl.multiple_of(c * LN_ROWS, LN_ROWS), LN_ROWS)
        x = ALPHA * r_ref[rows, :] + (m_ref[rows, :] + bias_ref[...])
        y = _layer_norm_rows(x, g_ref[...], b_ref[...])
        for o_ref in o_refs:
            o_ref[rows, :] = y.astype(o_ref.dtype)
        return carry

    lax.fori_loop(0, r_ref.shape[0] // LN_ROWS, body, 0, unroll=2)


def _add_ln_kernel(n_main, r_main_ref, r_tail_ref, m_ref, bias_ref, g_ref, b_ref, *o_refs):
    i = pl.program_id(0)

    @pl.when(i < n_main)
    def _():
        _add_ln_rows(r_main_ref, m_ref, bias_ref, g_ref, b_ref, o_refs)

    @pl.when(i >= n_main)
    def _():
        _add_ln_rows(r_tail_ref, m_ref, bias_ref, g_ref, b_ref, o_refs)


def _add_ln(r_main, r_tail, m, bias, g, b, out_dtypes, name):
    rows, d = m.shape
    tm = LN_TILE
    n_main = N_PROMPT // tm
    row_spec = pl.BlockSpec((tm, d), lambda i: (i, 0))
    vec_spec = pl.BlockSpec((1, d), lambda i: (0, 0))
    return pl.pallas_call(
        functools.partial(_add_ln_kernel, n_main),
        out_shape=[jax.ShapeDtypeStruct((rows, d), dt) for dt in out_dtypes],
        grid=(rows // tm,),
        in_specs=[pl.BlockSpec((tm, d), lambda i: (jnp.minimum(i, n_main - 1), 0)),
                  pl.BlockSpec((tm, d), lambda i: (jnp.maximum(i - n_main, 0), 0)),
                  row_spec, vec_spec, vec_spec, vec_spec],
        out_specs=[row_spec for _ in out_dtypes],
        compiler_params=_params(1),
        name=name,
    )(r_main, r_tail, m, bias, g, b)


def _final_ln_kernel(n_main, r_ref, m_ref, bias_ref, g_ref, b_ref, o_main_ref, o_sample_ref):
    i = pl.program_id(0)

    @pl.when(i < n_main)
    def _():
        _add_ln_rows(r_ref, m_ref, bias_ref, g_ref, b_ref, [o_main_ref])

    @pl.when(i == n_main)
    def _():
        _add_ln_rows(r_ref, m_ref, bias_ref, g_ref, b_ref, [o_sample_ref])


def _final_ln(r, m, bias, g, b):
    d = m.shape[1]
    tm = LN_TILE
    n_main = N_PROMPT // tm
    row_spec = pl.BlockSpec((tm, d), lambda i: (i, 0))
    vec_spec = pl.BlockSpec((1, d), lambda i: (0, 0))
    return pl.pallas_call(
        functools.partial(_final_ln_kernel, n_main),
        out_shape=[jax.ShapeDtypeStruct((N_PROMPT, d), F32),
                   jax.ShapeDtypeStruct((N_SAMPLE, d), F32)],
        grid=(n_main + 1,),
        in_specs=[row_spec, row_spec, vec_spec, vec_spec, vec_spec],
        out_specs=[pl.BlockSpec((tm, d), lambda i: (jnp.minimum(i, n_main - 1), 0)),
                   pl.BlockSpec((tm, d), lambda i: (0, 0))],
        compiler_params=_params(1),
        name="final_ln",
    )(r, m, bias, g, b)


def _conv_ln_silu(win_ref, wdw_ref, bdw_ref, g_ref, b_ref, o_ref, c_ref, sh_ref, t_rows):
    d = win_ref.shape[1]
    rh = min(CONV_ROWS, t_rows)
    sh_rows = sh_ref.shape[1]

    def col_body(cc, carry):
        cols = pl.ds(pl.multiple_of(cc * CONV_COLS, CONV_COLS), CONV_COLS)
        for r in range(1, SUBLANES):
            sh_ref[r - 1] = win_ref[r:r + sh_rows, cols]
        for r0 in range(0, t_rows, rh):
            acc = jnp.zeros((rh, CONV_COLS), F32) + bdw_ref[:, cols]
            for k in range(CONV_W):
                a, r = divmod(CONV_LEAD + k, SUBLANES)
                lo = r0 + a * SUBLANES
                tap = win_ref[lo:lo + rh, cols] if r == 0 else sh_ref[r - 1, lo:lo + rh, :]
                acc = acc + wdw_ref[k:k + 1, cols] * tap
            c_ref[r0:r0 + rh, cols] = acc
        return carry

    lax.fori_loop(0, d // CONV_COLS, col_body, 0)

    def ln_body(c, carry):
        rows = pl.ds(pl.multiple_of(c * LN_ROWS, LN_ROWS), LN_ROWS)
        y = _layer_norm_rows(c_ref[rows, :], g_ref[...], b_ref[...])
        o_ref[rows, :] = (y * jax.nn.sigmoid(y)).astype(o_ref.dtype)
        return carry

    n_ln = t_rows // LN_ROWS
    lax.fori_loop(0, n_ln, ln_body, 0, unroll=2 if n_ln % 2 == 0 else 1)


def _conv_prompt_kernel(prev_ref, cur_ref, first_ref, wdw_ref, bdw_ref, g_ref, b_ref,
                        o_ref, win_ref, c_ref, sh_ref):
    t_rows = cur_ref.shape[0]
    i = pl.program_id(1)

    @pl.when(i == 0)
    def _():
        win_ref[0:CONV_HALO, :] = first_ref[...]

    @pl.when(i > 0)
    def _():
        win_ref[0:CONV_HALO, :] = prev_ref[...]

    win_ref[CONV_HALO:CONV_HALO + t_rows, :] = cur_ref[...]
    _conv_ln_silu(win_ref, wdw_ref, bdw_ref, g_ref, b_ref, o_ref, c_ref, sh_ref, t_rows)


def _conv_stream_kernel(win_ref, wdw_ref, bdw_ref, g_ref, b_ref, o_ref, c_ref, sh_ref):
    _conv_ln_silu(win_ref, wdw_ref, bdw_ref, g_ref, b_ref, o_ref, c_ref, sh_ref, o_ref.shape[0])


def _conv_scratch(t, d):
    sh_rows = CONV_HALO + t - SUBLANES
    return [pltpu.VMEM((t, d), F32), pltpu.VMEM((SUBLANES - 1, sh_rows, CONV_COLS), F32)]


def _conv_prompt(g, first, wdw, bdw, ln_g, ln_b):
    d = g.shape[1]
    t = CONV_TILE
    tiles = SEQ // t
    halo_per_tile = t // CONV_HALO
    vec = lambda r: pl.BlockSpec((r, d), lambda b, i: (0, 0))
    return pl.pallas_call(
        _conv_prompt_kernel,
        out_shape=jax.ShapeDtypeStruct((N_PROMPT, d), BF16),
        grid=(BATCH, tiles),
        in_specs=[
            pl.BlockSpec((CONV_HALO, d),
                         lambda b, i: (jnp.maximum((b * tiles + i) * halo_per_tile - 1, 0), 0)),
            pl.BlockSpec((t, d), lambda b, i: (b * tiles + i, 0)),
            vec(CONV_HALO), vec(CONV_W), vec(1), vec(1), vec(1)],
        out_specs=pl.BlockSpec((t, d), lambda b, i: (b * tiles + i, 0)),
        scratch_shapes=[pltpu.VMEM((CONV_HALO + t, d), F32)] + _conv_scratch(t, d),
        compiler_params=_params(2),
        name="conv_prompt",
    )(g, g, first, wdw, bdw, ln_g, ln_b)


def _conv_streams(win, wdw, bdw, ln_g, ln_b):
    s, rows, d = win.shape
    t = rows - CONV_HALO
    vec = lambda r: pl.BlockSpec((r, d), lambda i: (0, 0))
    return pl.pallas_call(
        _conv_stream_kernel,
        out_shape=jax.ShapeDtypeStruct((s * t, d), BF16),
        grid=(s,),
        in_specs=[pl.BlockSpec((None, rows, d), lambda i: (i, 0, 0)),
                  vec(CONV_W), vec(1), vec(1), vec(1)],
        out_specs=pl.BlockSpec((t, d), lambda i: (i, 0)),
        scratch_shapes=_conv_scratch(t, d),
        compiler_params=_params(1),
        name="conv_streams",
    )(win, wdw, bdw, ln_g, ln_b)


def _suffix_sum_matrix(n):
    r = lax.broadcasted_iota(jnp.int32, (2 * n, n), 0) % n
    c = lax.broadcasted_iota(jnp.int32, (2 * n, n), 1)
    return jnp.where(r > c, 1.0, 0.0).astype(BF16)


def _bits_op(x, op, bits):
    u = lax.bitcast_convert_type(x, jnp.uint32)
    return lax.bitcast_convert_type(op(u, jnp.uint32(bits)), F32)


def _scores(q, k):
    return lax.dot_general(q, k, (((1,), (1,)), ((), ())), preferred_element_type=F32)


def _sb_terms(z2, mask):
    neg_abs = _bits_op(z2, jnp.bitwise_or, SIGN_BIT)
    t2 = jnp.log(1.0 + jnp.exp2(neg_abs)) * LOG2E
    log_beta = jnp.minimum(z2, 0.0) - t2
    neg_log_keep = z2 - log_beta
    if mask is not None:
        neg_log_keep = jnp.where(mask, neg_log_keep, 0.0)
    hi = _bits_op(neg_log_keep, jnp.bitwise_and, BF16_BITS)
    lo = neg_log_keep - hi
    hilo = jnp.concatenate([hi.astype(BF16), lo.astype(BF16)], axis=1)
    return log_beta, neg_log_keep[:, 0:1], hilo


def _sb_log_weights(log_beta, first_col, after, carry, mask):
    x = log_beta - after - carry
    if mask is not None:
        x = jnp.where(mask, x, NEG_BIG)
    return x, carry + (after[:, 0:1] + first_col)


def _sb_first_half(scores_fn, upper2, carry, mask):
    log_beta, first_col, hilo = _sb_terms(scores_fn(), mask)
    after = jnp.dot(hilo, upper2, preferred_element_type=F32)
    return _sb_log_weights(log_beta, first_col, after, carry, mask)


def _sb_weighted_values(x, v):
    return jnp.dot(jnp.exp2(x).astype(BF16), v, preferred_element_type=F32)


def _strict_lower_mask(nq, nk, period):
    r = lax.broadcasted_iota(jnp.int32, (nq, nk), 0) % period
    c = lax.broadcasted_iota(jnp.int32, (nq, nk), 1)
    return c < r


def _first_keys_mask(nq, nk, n_valid):
    return lax.broadcasted_iota(jnp.int32, (nq, nk), 1) < n_valid


def _attn_prompt_kernel(q_ref, k_ref, v_ref, km_ref, vm_ref, u_ref, us_ref, o_ref,
                        acc_ref, carry_ref, x_ref):
    qi = pl.program_id(2)
    blk = ATT_BLOCK
    nh = ATT_PROMPT_HEADS
    cols = [slice(h * HEAD_DIM, (h + 1) * HEAD_DIM) for h in range(nh)]

    def block_rows(j):
        return pl.ds(pl.multiple_of(j * blk, blk), blk)

    def step(v_pending, pending_keys, k_next, next_rows, upper2_ref, mask):
        next_keys = upper2_ref.shape[1]
        scores, mid = {}, {}
        for t in range(nh + 2):
            if t < nh:
                scores[t] = _scores(q_ref[:, cols[t]], k_next[next_rows, cols[t]])
            h = t - 1
            if 0 <= h < nh:
                acc_ref[h] += _sb_weighted_values(x_ref[h, :, :pending_keys], v_pending(h))
                log_beta, first_col, hilo = _sb_terms(scores.pop(h), mask)
                mid[h] = (log_beta, first_col,
                          jnp.dot(hilo, upper2_ref[...], preferred_element_type=F32))
            h = t - 2
            if 0 <= h < nh:
                x, carry = _sb_log_weights(*mid.pop(h), carry_ref[h], mask)
                x_ref[h, :, :next_keys] = x
                carry_ref[h] = carry

    diag_mask = _strict_lower_mask(blk, blk, blk)
    for h in range(nh):
        acc_ref[h] = jnp.zeros((blk, HEAD_DIM), F32)
        x, carry = _sb_first_half(
            lambda: _scores(q_ref[:, cols[h]], k_ref[block_rows(qi), cols[h]]),
            u_ref[...], jnp.zeros((blk, 1), F32), diag_mask)
        x_ref[h] = x
        carry_ref[h] = carry

    def earlier(p):
        step(lambda h: v_ref[block_rows(p), cols[h]], blk, k_ref, block_rows(p - 1), u_ref, None)

    odd = qi % 2
    pl.when(odd == 1)(lambda: earlier(qi))
    first_pending = qi - odd

    def body(t, carry):
        earlier(first_pending - 2 * t)
        earlier(first_pending - 2 * t - 1)
        return carry

    lax.fori_loop(0, first_pending // 2, body, 0)

    step(lambda h: v_ref[block_rows(0), cols[h]], blk, km_ref, slice(None), us_ref,
         _first_keys_mask(blk, SMALL_BLOCK, N_META))
    for h in range(nh):
        out = _sb_weighted_values(x_ref[h, :, :SMALL_BLOCK], vm_ref[:, cols[h]])
        o_ref[:, cols[h]] = (acc_ref[h] + out).astype(o_ref.dtype)


def _attn_prompt(q, kv, upper2, upper2_small):
    blk = ATT_BLOCK
    tiles = SEQ // blk
    hw = ATT_PROMPT_HEADS * HEAD_DIM
    hb = N_HEADS // ATT_PROMPT_HEADS
    meta_blk = META_ROW0 // SMALL_BLOCK
    const = lambda a: pl.BlockSpec(a.shape, lambda b, h, i: (0, 0))
    return pl.pallas_call(
        _attn_prompt_kernel,
        out_shape=jax.ShapeDtypeStruct((N_PROMPT, D_MODEL), BF16),
        grid=(BATCH, hb, tiles),
        in_specs=[
            pl.BlockSpec((blk, hw), lambda b, h, i: (b * tiles + i, h)),
            pl.BlockSpec((SEQ, hw), lambda b, h, i: (b, h)),
            pl.BlockSpec((SEQ, hw), lambda b, h, i: (b, hb + h)),
            pl.BlockSpec((SMALL_BLOCK, hw), lambda b, h, i: (meta_blk, h)),
            pl.BlockSpec((SMALL_BLOCK, hw), lambda b, h, i: (meta_blk, hb + h)),
            const(upper2), const(upper2_small)],
        out_specs=pl.BlockSpec((blk, hw), lambda b, h, i: (b * tiles + i, h)),
        scratch_shapes=[pltpu.VMEM((ATT_PROMPT_HEADS, blk, HEAD_DIM), F32),
                        pltpu.VMEM((ATT_PROMPT_HEADS, blk, 1), F32),
                        pltpu.VMEM((ATT_PROMPT_HEADS, blk, blk), F32)],
        compiler_params=_params(3),
        name="attn_prompt",
    )(q, kv, kv, kv, kv, upper2, upper2_small)


def _head_rows(ref3, start, size, h):
    keys, nh, hd = ref3.shape
    ref2 = ref3.reshape(keys * nh, hd)
    return ref2[pl.ds(start * nh + h, size, stride=nh), :]


def _attn_sample_kernel(q_ref, kn_ref, vn_ref, kc_ref, vc_ref, km_ref, vm_ref, u_ref, us_ref,
                        o_ref):
    blk = ATT_BLOCK
    nh = ATT_SAMPLE_HEADS
    nq = nh * DEC_SEQ
    cols = [slice(h * HEAD_DIM, (h + 1) * HEAD_DIM) for h in range(nh)]
    qrows = [slice(h * DEC_SEQ, (h + 1) * DEC_SEQ) for h in range(nh)]
    fill = jnp.zeros((SMALL_BLOCK - DEC_SEQ, HEAD_DIM), BF16)

    def block(k_of, v_of, u, mask):
        scores = lambda: jnp.concatenate(
            [_scores(q_ref[:, cols[h]], k_of(h)) for h in range(nh)], axis=0)
        return scores, v_of, u, mask

    def cached(ref3, c):
        return lambda h: _head_rows(ref3, c * blk, blk, h).astype(BF16)

    blocks = [block(lambda h: jnp.concatenate([kn_ref[:, cols[h]], fill], axis=0),
                    lambda h: jnp.concatenate([vn_ref[:, cols[h]], fill], axis=0),
                    us_ref, _strict_lower_mask(nq, SMALL_BLOCK, DEC_SEQ))]
    for c in reversed(range(PAST_LEN // blk)):
        blocks.append(block(cached(kc_ref, c), cached(vc_ref, c), u_ref, None))
    blocks.append(block(lambda h: km_ref[:, cols[h]], lambda h: vm_ref[:, cols[h]],
                        us_ref, _first_keys_mask(nq, SMALL_BLOCK, N_META)))

    acc = jnp.zeros((nq, HEAD_DIM), F32)
    pending, carry = _sb_first_half(blocks[0][0], blocks[0][2][...], jnp.zeros((nq, 1), F32),
                                    blocks[0][3])
    for b in range(len(blocks)):
        nxt = None
        if b + 1 < len(blocks):
            nxt, carry = _sb_first_half(blocks[b + 1][0], blocks[b + 1][2][...], carry,
                                        blocks[b + 1][3])
        a = jnp.exp2(pending).astype(BF16)
        acc = acc + jnp.concatenate(
            [jnp.dot(a[qrows[h]], blocks[b][1](h), preferred_element_type=F32)
             for h in range(nh)], axis=0)
        pending = nxt
    for h in range(nh):
        o_ref[:, cols[h]] = acc[qrows[h]].astype(o_ref.dtype)


def _attn_sample(q, kv, cache_k, cache_v, upper2, upper2_small):
    nh = ATT_SAMPLE_HEADS
    hw = nh * HEAD_DIM
    hb = N_HEADS // nh
    row_blk0 = SAMPLE_ROW0 // DEC_SEQ
    meta_blk = META_ROW0 // SMALL_BLOCK
    new = lambda off: pl.BlockSpec((DEC_SEQ, hw), lambda s, h: (row_blk0 + s, off + h))
    meta = lambda off: pl.BlockSpec((SMALL_BLOCK, hw), lambda s, h: (meta_blk, off + h))
    cache = pl.BlockSpec((None, PAST_LEN, nh, HEAD_DIM), lambda s, h: (s, 0, h, 0))
    const = lambda a: pl.BlockSpec(a.shape, lambda s, h: (0, 0))
    return pl.pallas_call(
        _attn_sample_kernel,
        out_shape=jax.ShapeDtypeStruct((N_SAMPLE, D_MODEL), BF16),
        grid=(DEC_BATCH, hb),
        in_specs=[new(0), new(0), new(hb), cache, cache, meta(0), meta(hb),
                  const(upper2), const(upper2_small)],
        out_specs=pl.BlockSpec((DEC_SEQ, hw), lambda s, h: (s, h)),
        compiler_params=_params(2),
        name="attn_sample",
    )(q, kv, kv, cache_k, cache_v, kv, kv, upper2, upper2_small)


def kernel(x_prompt, x_sample, state_conv, cache_k, cache_v, meta, a_w_pw1, a_b_pw1, a_w_dw, a_b_dw, a_ln_g, a_ln_b, a_w_pw2, a_b_pw2, w_kv, w_q, w_o, ln_mix_g, ln_mix_b, ln_ffn_g, ln_ffn_b, w_gate, w_up, w_down):
    d = D_MODEL
    n_pad = N_ROWS - META_ROW0 - N_META
    w_down_b = w_down.astype(BF16)
    zero_vec = jnp.zeros((1, d), F32)
    xp = x_prompt.reshape(N_PROMPT, d)
    x_tail = jnp.concatenate([x_sample.reshape(N_SAMPLE, d), meta, jnp.zeros((n_pad, d), F32)],
                             axis=0)
    tail_pad = jnp.zeros((N_TAIL - N_SAMPLE, d), BF16)

    def ffn(r_main, r_tail, xb, l):
        h = _swiglu(xb, w_gate, w_up, l)
        (y,) = _matmul(h, w_down_b, [F32], DOWN_COL_TILE, "ffn_down", layer=l)
        return r_main, r_tail, y, zero_vec, ln_ffn_g[l][None], ln_ffn_b[l][None]

    g = _conv_glu((xp.astype(BF16), x_tail.astype(BF16)), a_w_pw1[0], a_b_pw1[0][None])
    g_sample = g[SAMPLE_ROW0:META_ROW0].reshape(DEC_BATCH, DEC_SEQ, d)
    g_meta = g[META_ROW0:META_ROW0 + N_META]
    first = jnp.concatenate([jnp.zeros((CONV_HALO - N_META, d), F32), g_meta], axis=0)
    conv_args = (a_w_dw[0], a_b_dw[0][None], a_ln_g[0][None], a_ln_b[0][None])
    c_prompt = _conv_prompt(g, first, *conv_args)
    hist = jnp.concatenate([state_conv[0], jnp.zeros((1, CONV_W - 1, d), F32)], axis=0)
    new = jnp.concatenate([g_sample, g_meta[None]], axis=0)
    win = jnp.concatenate([jnp.zeros((DEC_BATCH + 1, CONV_LEAD, d), F32), hist, new], axis=1)
    c_tail = jnp.concatenate([_conv_streams(win, *conv_args), jnp.zeros((n_pad, d), BF16)],
                             axis=0)
    (m,) = _matmul((c_prompt, c_tail), a_w_pw2[0].astype(BF16), [F32], WIDE_COL_TILE,
                   "conv_pw2")
    x1, x1b = _add_ln(xp, x_tail, m, a_b_pw2[0][None], ln_mix_g[0][None], ln_mix_b[0][None],
                      [F32, BF16], "mix_ln")
    x2, x2b = _add_ln(*ffn(x1, x1[N_PROMPT:], x1b, 0), [F32, BF16], "ffn_ln")

    kvb, k_s, v_s, k_p, v_p = _kv_proj(x2b, w_kv.astype(BF16))
    (qb,) = _matmul(x2b, w_q[0].astype(BF16), [BF16], WIDE_COL_TILE, "q_proj",
                    scale=SCORE_SCALE)
    upper2 = _suffix_sum_matrix(ATT_BLOCK)
    upper2_small = _suffix_sum_matrix(SMALL_BLOCK)
    o_prompt = _attn_prompt(qb, kvb, upper2, upper2_small)
    o_sample = _attn_sample(qb, kvb, cache_k, cache_v, upper2, upper2_small)
    o_tail = jnp.concatenate([o_sample, tail_pad], axis=0)
    (m,) = _matmul((o_prompt, o_tail), w_o[0].astype(BF16), [F32], WIDE_COL_TILE, "o_proj")
    x3, x3b = _add_ln(x2, x2[N_PROMPT:], m, zero_vec, ln_mix_g[1][None], ln_mix_b[1][None],
                      [F32, BF16], "mix_ln")
    _, _, y, bias, ln_g, ln_b = ffn(x3, None, x3b, 1)
    y_prompt, y_sample = _final_ln(x3, y, bias, ln_g, ln_b)

    y_prompt = y_prompt.reshape(BATCH, SEQ, d)
    y_sample = y_sample.reshape(DEC_BATCH, DEC_SEQ, d)
    n_ctx = CONV_W - 1
    state_conv_prompt = jnp.stack(
        [g[(b + 1) * SEQ - n_ctx:(b + 1) * SEQ] for b in range(BATCH)], axis=0)[None]
    state_conv_sample = jnp.concatenate([state_conv[0][:, DEC_SEQ:], g_sample], axis=1)[None]

    sample_shape = (DEC_BATCH, DEC_SEQ, N_HEADS, HEAD_DIM)
    return (y_prompt, y_sample, state_conv_prompt, state_conv_sample,
            k_p, v_p, k_s.reshape(sample_shape), v_s.reshape(sample_shape))
```

```python
import functools

import jax
import jax.numpy as jnp
from jax import lax
from jax.experimental import pallas as pl
from jax.experimental.pallas import tpu as pltpu

D_MODEL = 4096
BATCH = 2
SEQ = 4096
DEC_BATCH = 16
DEC_SEQ = 16
PAST_LEN = 1024
N_META = 16
CONV_W = 31
HEAD_DIM = 128
D_FF = 11008
DEPTH = 2
LN_EPS = 1e-5

N_HEADS = D_MODEL // HEAD_DIM
ALPHA = (2.0 * DEPTH) ** 0.25
SCALE = HEAD_DIM ** -0.5
LOG2E = 1.4426950408889634
SCORE_SCALE = SCALE * LOG2E

LANES = 128
SUBLANES = 8
VMEM_LIMIT = 56 * 1024 * 1024
ROW_TILE = 512
STREAM_PIECE = 256
DOWN_COL_TILE = 512
GU_COL_TILE = 1024
GU_CHUNK = 256
WIDE_COL_TILE = min(2048, D_MODEL)
VMEM_LIMIT_PW1 = 60 * 1024 * 1024
LN_TILE = 256
LN_ROWS = 16
CONV_TILE = 256
CONV_ROWS = 64
CONV_COLS = 512
CONV_HALO = 32
ATT_BLOCK = 256
SMALL_BLOCK = 128
ATT_PROMPT_HEADS = 8
ATT_SAMPLE_HEADS = SUBLANES

N_PROMPT = BATCH * SEQ
N_SAMPLE = DEC_BATCH * DEC_SEQ
SAMPLE_ROW0 = N_PROMPT
META_ROW0 = N_PROMPT + N_SAMPLE
N_ROWS = -(-(META_ROW0 + N_META) // ROW_TILE) * ROW_TILE
N_TAIL = N_ROWS - N_PROMPT
CONV_LEAD = CONV_HALO - (CONV_W - 1)
SIGN_BIT = 0x80000000
BF16_BITS = 0xFFFF0000
NEG_BIG = -1e30

assert N_PROMPT % ROW_TILE == 0 and N_TAIL % ROW_TILE == 0
assert N_PROMPT % LN_TILE == 0 and N_SAMPLE == LN_TILE
assert META_ROW0 % SMALL_BLOCK == 0 and SEQ % ATT_BLOCK == 0 and PAST_LEN % ATT_BLOCK == 0

F32 = jnp.float32
BF16 = jnp.bfloat16


def _params(n_axes, vmem_limit=VMEM_LIMIT):
    return pltpu.CompilerParams(
        dimension_semantics=("arbitrary",) * n_axes, vmem_limit_bytes=vmem_limit)


def _stream_weights(srcs, n_cols_total, w16_ref, stage_ref, sem):
    j, i = pl.program_id(0), pl.program_id(1)
    n_tiles, n_i = pl.num_programs(0), pl.num_programs(1)
    n_w, k, bn = w16_ref.shape[1:]
    chunk = stage_ref.shape[2]
    n_chunks = k // chunk
    n_pieces = bn // STREAM_PIECE
    step = j * n_i + i

    def copies(tile, c, parity):
        out = []
        for w in range(n_w):
            for p in range(n_pieces):
                col0 = jnp.minimum(tile * bn + p * STREAM_PIECE, n_cols_total - STREAM_PIECE)
                src = srcs[w](pl.ds(c * chunk, chunk), pl.ds(col0, STREAM_PIECE))
                dst = stage_ref.at[parity, w, :, pl.ds(p * STREAM_PIECE, STREAM_PIECE)]
                out.append(pltpu.make_async_copy(src, dst, sem.at[parity, w, p]))
        return out

    def cast(tile, c, parity):
        rows = pl.ds(pl.multiple_of(c * chunk, 16), chunk)
        for w in range(n_w):
            w16_ref[tile % 2, w, rows, :] = stage_ref[parity, w].astype(BF16)

    def wanted(at_j, at_i):
        return jnp.minimum(at_j + 1, n_tiles - 1), jnp.minimum(at_i, n_chunks - 1)

    @pl.when(step == 0)
    def _():
        def body(c, carry):
            for cp in copies(0, c, c % 2):
                cp.start()
            for cp in copies(0, c, c % 2):
                cp.wait()
            cast(0, c, c % 2)
            return carry

        lax.fori_loop(0, n_chunks, body, 0)
        for cp in copies(*wanted(0, 0), 0):
            cp.start()

    tile, c = wanted(j, i)
    for cp in copies(tile, c, step % 2):
        cp.wait()
    cast(tile, c, step % 2)

    def start_next():
        wrap = i + 1 == n_i
        tile2, c2 = wanted(jnp.where(wrap, j + 1, j), jnp.where(wrap, 0, i + 1))
        for cp in copies(tile2, c2, (step + 1) % 2):
            cp.start()

    return lambda: pl.when(step + 1 < n_tiles * n_i)(start_next)


def _stream_scratch(n_w, k, bn, chunk):
    assert k % chunk == 0 and chunk % 16 == 0 and bn % STREAM_PIECE == 0
    return [pltpu.VMEM((2, n_w, k, bn), BF16), pltpu.VMEM((2, n_w, chunk, bn), F32),
            pltpu.SemaphoreType.DMA((2, n_w, bn // STREAM_PIECE))]


def _row_specs(x, bm):
    if not isinstance(x, tuple):
        return [x], [pl.BlockSpec((bm, x.shape[1]), lambda j, i: (i, 0))], None
    main, tail = x
    n_main = main.shape[0] // bm
    k = main.shape[1]
    return ([main, tail],
            [pl.BlockSpec((bm, k), lambda j, i: (jnp.minimum(i, n_main - 1), 0)),
             pl.BlockSpec((bm, k), lambda j, i: (jnp.maximum(i - n_main, 0), 0),
                          pipeline_mode=pl.Buffered(1))],
            n_main)


def _n_rows(x):
    return sum(a.shape[0] for a in x) if isinstance(x, tuple) else x.shape[0]


def _on_row_source(n_main, x_refs, body):
    if n_main is None:
        body(x_refs[0])
        return
    i = pl.program_id(1)
    pl.when(i < n_main)(lambda: body(x_refs[0]))
    pl.when(i >= n_main)(lambda: body(x_refs[1]))


def _mm_kernel(n_main, scale, *refs):
    n_x = 1 if n_main is None else 2
    x_refs, (w_ref, *o_refs) = refs[:n_x], refs[n_x:]

    def body(x_ref):
        acc = jnp.dot(x_ref[...], w_ref[...], preferred_element_type=F32)
        if scale is not None:
            acc = acc * scale
        for o_ref in o_refs:
            o_ref[...] = acc.astype(o_ref.dtype)

    _on_row_source(n_main, x_refs, body)


def _matmul(x, w, out_dtypes, bn, name, layer=None, scale=None):
    arrays, x_specs, n_main = _row_specs(x, ROW_TILE)
    m, bm = _n_rows(x), ROW_TILE
    k, n = w.shape[-2:]
    if layer is None:
        w_spec = pl.BlockSpec((k, bn), lambda j, i: (0, j))
    else:
        w_spec = pl.BlockSpec((None, k, bn), lambda j, i: (layer, 0, j))
    return pl.pallas_call(
        functools.partial(_mm_kernel, n_main, scale),
        out_shape=[jax.ShapeDtypeStruct((m, n), dt) for dt in out_dtypes],
        grid=(n // bn, m // bm),
        in_specs=x_specs + [w_spec],
        out_specs=[pl.BlockSpec((bm, bn), lambda j, i: (i, j)) for _ in out_dtypes],
        compiler_params=_params(2),
        name=name,
    )(*arrays, w)


def _mm_stream_kernel(layer, n_cols_total, x_ref, w_hbm, o_ref, w16_ref, stage_ref, sem):
    start_next = _stream_weights([lambda r, c: w_hbm.at[layer, r, c]], n_cols_total,
                                 w16_ref, stage_ref, sem)
    j = pl.program_id(0)
    o_ref[...] = jnp.dot(x_ref[...], w16_ref[j % 2, 0],
                         preferred_element_type=F32).astype(o_ref.dtype)
    start_next()


def _matmul_stream(x, w, layer, out_dtype, bn, name):
    m, k = x.shape
    n = w.shape[2]
    bm = ROW_TILE
    n_chunks = max(c for c in range(1, m // bm + 1) if k % (16 * c) == 0)
    return pl.pallas_call(
        functools.partial(_mm_stream_kernel, layer, n),
        out_shape=jax.ShapeDtypeStruct((m, n), out_dtype),
        grid=(n // bn, m // bm),
        in_specs=[pl.BlockSpec((bm, k), lambda j, i: (i, 0)),
                  pl.BlockSpec(memory_space=pl.ANY)],
        out_specs=pl.BlockSpec((bm, bn), lambda j, i: (i, j)),
        scratch_shapes=_stream_scratch(1, k, bn, k // n_chunks),
        compiler_params=_params(2),
        name=name,
    )(x, w)


def _glu_kernel(n_main, n_half, *refs):
    n_x = 1 if n_main is None else 2
    x_refs, (w_hbm, ba_ref, bb_ref, o_ref, w16_ref, stage_ref, sem) = refs[:n_x], refs[n_x:]
    srcs = [lambda r, c: w_hbm.at[r, c],
            lambda r, c: w_hbm.at[r, pl.ds(c.start + n_half, c.size)]]
    start_next = _stream_weights(srcs, n_half, w16_ref, stage_ref, sem)
    slot = pl.program_id(0) % 2

    def body(x_ref):
        x = x_ref[...]
        a = jnp.dot(x, w16_ref[slot, 0], preferred_element_type=F32) + ba_ref[...]
        b = jnp.dot(x, w16_ref[slot, 1], preferred_element_type=F32) + bb_ref[...]
        o_ref[...] = (a * jax.nn.sigmoid(b)).astype(o_ref.dtype)

    _on_row_source(n_main, x_refs, body)
    start_next()


def _swiglu_kernel(layer, x_ref, wg_hbm, wu_hbm, o_ref, w16_ref, stage_ref, sem):
    srcs = [lambda r, c: wg_hbm.at[layer, r, c], lambda r, c: wu_hbm.at[layer, r, c]]
    start_next = _stream_weights(srcs, D_FF, w16_ref, stage_ref, sem)
    slot = pl.program_id(0) % 2
    x = x_ref[...]
    g = jnp.dot(x, w16_ref[slot, 0], preferred_element_type=F32)
    u = jnp.dot(x, w16_ref[slot, 1], preferred_element_type=F32)
    o_ref[...] = (g * jax.nn.sigmoid(g) * u).astype(o_ref.dtype)
    start_next()


def _conv_glu(x, w1, b1):
    arrays, x_specs, n_main = _row_specs(x, ROW_TILE)
    m, bm, bn = _n_rows(x), ROW_TILE, GU_COL_TILE
    k = w1.shape[0]
    n = w1.shape[1] // 2
    nb = n // bn
    assert k // GU_CHUNK <= m // bm
    return pl.pallas_call(
        functools.partial(_glu_kernel, n_main, n),
        out_shape=jax.ShapeDtypeStruct((m, n), F32),
        grid=(nb, m // bm),
        in_specs=x_specs + [
            pl.BlockSpec(memory_space=pl.ANY),
            pl.BlockSpec((1, bn), lambda j, i: (0, j)),
            pl.BlockSpec((1, bn), lambda j, i: (0, j + nb))],
        out_specs=pl.BlockSpec((bm, bn), lambda j, i: (i, j)),
        scratch_shapes=_stream_scratch(2, k, bn, GU_CHUNK),
        compiler_params=_params(2, VMEM_LIMIT_PW1),
        name="conv_pw1_glu",
    )(*arrays, w1, b1, b1)


def _swiglu(x, wg, wu, layer):
    m, k = x.shape
    bm, bn = ROW_TILE, GU_COL_TILE
    assert k // GU_CHUNK <= m // bm
    hbm_spec = pl.BlockSpec(memory_space=pl.ANY)
    return pl.pallas_call(
        functools.partial(_swiglu_kernel, layer),
        out_shape=jax.ShapeDtypeStruct((m, D_FF), BF16),
        grid=(pl.cdiv(D_FF, bn), m // bm),
        in_specs=[pl.BlockSpec((bm, k), lambda j, i: (i, 0)), hbm_spec, hbm_spec],
        out_specs=pl.BlockSpec((bm, bn), lambda j, i: (i, j)),
        scratch_shapes=_stream_scratch(2, k, bn, GU_CHUNK),
        compiler_params=_params(2),
        name="ffn_gate_up",
    )(x, wg, wu)


KV_HEADS = SUBLANES
KV_COLS = KV_HEADS * HEAD_DIM
PROMPT_TILES = N_PROMPT // ROW_TILE
assert PROMPT_TILES % 2 == 0 and SEQ % ROW_TILE == 0 and D_MODEL % KV_COLS == 0


def _store_heads(dst_ref, acc):
    rows, nh, hd = dst_ref.shape
    flat = dst_ref.reshape(rows * nh, hd)
    for h in range(nh):
        flat[pl.ds(h, rows, stride=nh), :] = acc[:, h * hd:(h + 1) * hd]


def _kv_kernel(x_ref, w_ref, kvb_ref, ks_ref, vs_ref, kp_ref, vp_ref,
               stage0_ref, stage1_ref, meta_stage_ref, sem, meta_sem):
    j, i = pl.program_id(0), pl.program_id(1)
    n_kb = D_MODEL // KV_COLS
    tiles_per_stream = SEQ // ROW_TILE
    acc = jnp.dot(x_ref[...], w_ref[...], preferred_element_type=F32)
    kvb_ref[...] = acc.astype(BF16)
    h0 = (j % n_kb) * KV_HEADS

    def prompt_copy(dst_ref, stage_ref, slot, tile):
        b = tile // tiles_per_stream
        r0 = N_META + (tile % tiles_per_stream) * ROW_TILE
        dst = dst_ref.at[b, pl.ds(r0, ROW_TILE), pl.ds(h0, KV_HEADS), :]
        return pltpu.make_async_copy(stage_ref, dst, sem.at[slot])

    def emit(dst_ref, sample_ref):
        for slot, stage_ref in enumerate((stage0_ref, stage1_ref)):
            @pl.when((i < PROMPT_TILES) & (i % 2 == slot))
            def _():
                @pl.when(i >= 2)
                def _():
                    prompt_copy(dst_ref, stage_ref, slot, i - 2).wait()

                _store_heads(stage_ref, acc)
                prompt_copy(dst_ref, stage_ref, slot, i).start()

        @pl.when(i == PROMPT_TILES)
        def _():
            prompt_copy(dst_ref, stage0_ref, 0, PROMPT_TILES - 2).wait()
            prompt_copy(dst_ref, stage1_ref, 1, PROMPT_TILES - 1).wait()
            _store_heads(sample_ref, acc[:N_SAMPLE])
            _store_heads(meta_stage_ref, acc[N_SAMPLE:N_SAMPLE + N_META])
            copies = [pltpu.make_async_copy(
                meta_stage_ref, dst_ref.at[b, pl.ds(0, N_META), pl.ds(h0, KV_HEADS), :],
                meta_sem.at[b]) for b in range(BATCH)]
            for c in copies:
                c.start()
            for c in copies:
                c.wait()

    pl.when(j < n_kb)(lambda: emit(kp_ref, ks_ref))
    pl.when(j >= n_kb)(lambda: emit(vp_ref, vs_ref))


def _kv_proj(x, w_kv):
    m, k = x.shape
    bm, bn = ROW_TILE, KV_COLS
    n_kb = D_MODEL // bn
    assert m // bm == PROMPT_TILES + 1
    small = jax.ShapeDtypeStruct((N_SAMPLE, N_HEADS, HEAD_DIM), F32)
    big = jax.ShapeDtypeStruct((BATCH, N_META + SEQ, N_HEADS, HEAD_DIM), F32)
    return pl.pallas_call(
        _kv_kernel,
        out_shape=[jax.ShapeDtypeStruct((m, 2 * D_MODEL), BF16), small, small, big, big],
        grid=(2 * n_kb, m // bm),
        in_specs=[pl.BlockSpec((bm, k), lambda j, i: (i, 0)),
                  pl.BlockSpec((k, bn), lambda j, i: (0, j))],
        out_specs=[
            pl.BlockSpec((bm, bn), lambda j, i: (i, j)),
            pl.BlockSpec((N_SAMPLE, KV_HEADS, HEAD_DIM),
                         lambda j, i: (0, jnp.minimum(j, n_kb - 1), 0)),
            pl.BlockSpec((N_SAMPLE, KV_HEADS, HEAD_DIM),
                         lambda j, i: (0, jnp.maximum(j - n_kb, 0), 0)),
            pl.BlockSpec(memory_space=pl.ANY),
            pl.BlockSpec(memory_space=pl.ANY)],
        scratch_shapes=[pltpu.VMEM((bm, KV_HEADS, HEAD_DIM), F32),
                        pltpu.VMEM((bm, KV_HEADS, HEAD_DIM), F32),
                        pltpu.VMEM((N_META, KV_HEADS, HEAD_DIM), F32),
                        pltpu.SemaphoreType.DMA((2,)),
                        pltpu.SemaphoreType.DMA((BATCH,))],
        compiler_params=_params(2),
        name="kv_proj",
    )(x, w_kv)


def _layer_norm_rows(x, g, b):
    mu = jnp.mean(x, axis=-1, keepdims=True)
    xc = x - mu
    var = jnp.mean(xc * xc, axis=-1, keepdims=True)
    return xc * lax.rsqrt(var + LN_EPS) * g + b


def _add_ln_rows(r_ref, m_ref, bias_ref, g_ref, b_ref, o_refs):
    def body(c, carry):
        rows = pl.ds(pl.multiple_of(c * LN_ROWS, LN_ROWS), LN_ROWS)
        x = ALPHA * r_ref[rows, :] + (m_ref[rows, :] + bias_ref[...])
        y = _layer_norm_rows(x, g_ref[...], b_ref[...])
        for o_ref in o_refs:
            o_ref[rows, :] = y.astype(o_ref.dtype)
        return carry

    lax.fori_loop(0, r_ref.shape[0] // LN_ROWS, body, 0, unroll=2)


def _add_ln_kernel(n_main, r_main_ref, r_tail_ref, m_ref, bias_ref, g_ref, b_ref, *o_refs):
    i = pl.program_id(0)

    @pl.when(i < n_main)
    def _():
        _add_ln_rows(r_main_ref, m_ref, bias_ref, g_ref, b_ref, o_refs)

    @pl.when(i >= n_main)
    def _():
        _add_ln_rows(r_tail_ref, m_ref, bias_ref, g_ref, b_ref, o_refs)


def _add_ln(r_main, r_tail, m, bias, g, b, out_dtypes, name):
    rows, d = m.shape
    tm = LN_TILE
    n_main = N_PROMPT // tm
    row_spec = pl.BlockSpec((tm, d), lambda i: (i, 0))
    vec_spec = pl.BlockSpec((1, d), lambda i: (0, 0))
    return pl.pallas_call(
        functools.partial(_add_ln_kernel, n_main),
        out_shape=[jax.ShapeDtypeStruct((rows, d), dt) for dt in out_dtypes],
        grid=(rows // tm,),
        in_specs=[pl.BlockSpec((tm, d), lambda i: (jnp.minimum(i, n_main - 1), 0)),
                  pl.BlockSpec((tm, d), lambda i: (jnp.maximum(i - n_main, 0), 0)),
                  row_spec, vec_spec, vec_spec, vec_spec],
        out_specs=[row_spec for _ in out_dtypes],
        compiler_params=_params(1),
        name=name,
    )(r_main, r_tail, m, bias, g, b)


def _final_ln_kernel(n_main, r_ref, m_ref, bias_ref, g_ref, b_ref, o_main_ref, o_sample_ref):
    i = pl.program_id(0)

    @pl.when(i < n_main)
    def _():
        _add_ln_rows(r_ref, m_ref, bias_ref, g_ref, b_ref, [o_main_ref])

    @pl.when(i == n_main)
    def _():
        _add_ln_rows(r_ref, m_ref, bias_ref, g_ref, b_ref, [o_sample_ref])


def _final_ln(r, m, bias, g, b):
    d = m.shape[1]
    tm = LN_TILE
    n_main = N_PROMPT // tm
    row_spec = pl.BlockSpec((tm, d), lambda i: (i, 0))
    vec_spec = pl.BlockSpec((1, d), lambda i: (0, 0))
    return pl.pallas_call(
        functools.partial(_final_ln_kernel, n_main),
        out_shape=[jax.ShapeDtypeStruct((N_PROMPT, d), F32),
                   jax.ShapeDtypeStruct((N_SAMPLE, d), F32)],
        grid=(n_main + 1,),
        in_specs=[row_spec, row_spec, vec_spec, vec_spec, vec_spec],
        out_specs=[pl.BlockSpec((tm, d), lambda i: (jnp.minimum(i, n_main - 1), 0)),
                   pl.BlockSpec((tm, d), lambda i: (0, 0))],
        compiler_params=_params(1),
        name="final_ln",
    )(r, m, bias, g, b)


def _conv_ln_silu(win_ref, wdw_ref, bdw_ref, g_ref, b_ref, o_ref, c_ref, sh_ref, t_rows):
    d = win_ref.shape[1]
    rh = min(CONV_ROWS, t_rows)
    sh_rows = sh_ref.shape[1]

    def col_body(cc, carry):
        cols = pl.ds(pl.multiple_of(cc * CONV_COLS, CONV_COLS), CONV_COLS)
        for r in range(1, SUBLANES):
            sh_ref[r - 1] = win_ref[r:r + sh_rows, cols]
        for r0 in range(0, t_rows, rh):
            acc = jnp.zeros((rh, CONV_COLS), F32) + bdw_ref[:, cols]
            for k in range(CONV_W):
                a, r = divmod(CONV_LEAD + k, SUBLANES)
                lo = r0 + a * SUBLANES
                tap = win_ref[lo:lo + rh, cols] if r == 0 else sh_ref[r - 1, lo:lo + rh, :]
                acc = acc + wdw_ref[k:k + 1, cols] * tap
            c_ref[r0:r0 + rh, cols] = acc
        return carry

    lax.fori_loop(0, d // CONV_COLS, col_body, 0)

    def ln_body(c, carry):
        rows = pl.ds(pl.multiple_of(c * LN_ROWS, LN_ROWS), LN_ROWS)
        y = _layer_norm_rows(c_ref[rows, :], g_ref[...], b_ref[...])
        o_ref[rows, :] = (y * jax.nn.sigmoid(y)).astype(o_ref.dtype)
        return carry

    n_ln = t_rows // LN_ROWS
    lax.fori_loop(0, n_ln, ln_body, 0, unroll=2 if n_ln % 2 == 0 else 1)


def _conv_prompt_kernel(prev_ref, cur_ref, first_ref, wdw_ref, bdw_ref, g_ref, b_ref,
                        o_ref, win_ref, c_ref, sh_ref):
    t_rows = cur_ref.shape[0]
    i = pl.program_id(1)

    @pl.when(i == 0)
    def _():
        win_ref[0:CONV_HALO, :] = first_ref[...]

    @pl.when(i > 0)
    def _():
        win_ref[0:CONV_HALO, :] = prev_ref[...]

    win_ref[CONV_HALO:CONV_HALO + t_rows, :] = cur_ref[...]
    _conv_ln_silu(win_ref, wdw_ref, bdw_ref, g_ref, b_ref, o_ref, c_ref, sh_ref, t_rows)


def _conv_stream_kernel(win_ref, wdw_ref, bdw_ref, g_ref, b_ref, o_ref, c_ref, sh_ref):
    _conv_ln_silu(win_ref, wdw_ref, bdw_ref, g_ref, b_ref, o_ref, c_ref, sh_ref, o_ref.shape[0])


def _conv_scratch(t, d):
    sh_rows = CONV_HALO + t - SUBLANES
    return [pltpu.VMEM((t, d), F32), pltpu.VMEM((SUBLANES - 1, sh_rows, CONV_COLS), F32)]


def _conv_prompt(g, first, wdw, bdw, ln_g, ln_b):
    d = g.shape[1]
    t = CONV_TILE
    tiles = SEQ // t
    halo_per_tile = t // CONV_HALO
    vec = lambda r: pl.BlockSpec((r, d), lambda b, i: (0, 0))
    return pl.pallas_call(
        _conv_prompt_kernel,
        out_shape=jax.ShapeDtypeStruct((N_PROMPT, d), BF16),
        grid=(BATCH, tiles),
        in_specs=[
            pl.BlockSpec((CONV_HALO, d),
                         lambda b, i: (jnp.maximum((b * tiles + i) * halo_per_tile - 1, 0), 0)),
            pl.BlockSpec((t, d), lambda b, i: (b * tiles + i, 0)),
            vec(CONV_HALO), vec(CONV_W), vec(1), vec(1), vec(1)],
        out_specs=pl.BlockSpec((t, d), lambda b, i: (b * tiles + i, 0)),
        scratch_shapes=[pltpu.VMEM((CONV_HALO + t, d), F32)] + _conv_scratch(t, d),
        compiler_params=_params(2),
        name="conv_prompt",
    )(g, g, first, wdw, bdw, ln_g, ln_b)


def _conv_streams(win, wdw, bdw, ln_g, ln_b):
    s, rows, d = win.shape
    t = rows - CONV_HALO
    vec = lambda r: pl.BlockSpec((r, d), lambda i: (0, 0))
    return pl.pallas_call(
        _conv_stream_kernel,
        out_shape=jax.ShapeDtypeStruct((s * t, d), BF16),
        grid=(s,),
        in_specs=[pl.BlockSpec((None, rows, d), lambda i: (i, 0, 0)),
                  vec(CONV_W), vec(1), vec(1), vec(1)],
        out_specs=pl.BlockSpec((t, d), lambda i: (i, 0)),
        scratch_shapes=_conv_scratch(t, d),
        compiler_params=_params(1),
        name="conv_streams",
    )(win, wdw, bdw, ln_g, ln_b)


def _suffix_sum_matrix(n):
    r = lax.broadcasted_iota(jnp.int32, (2 * n, n), 0) % n
    c = lax.broadcasted_iota(jnp.int32, (2 * n, n), 1)
    return jnp.where(r > c, 1.0, 0.0).astype(BF16)


def _bits_op(x, op, bits):
    u = lax.bitcast_convert_type(x, jnp.uint32)
    return lax.bitcast_convert_type(op(u, jnp.uint32(bits)), F32)


def _scores(q, k):
    return lax.dot_general(q, k, (((1,), (1,)), ((), ())), preferred_element_type=F32)


def _sb_terms(z2, mask):
    neg_abs = _bits_op(z2, jnp.bitwise_or, SIGN_BIT)
    t2 = jnp.log(1.0 + jnp.exp2(neg_abs)) * LOG2E
    log_beta = jnp.minimum(z2, 0.0) - t2
    neg_log_keep = z2 - log_beta
    if mask is not None:
        neg_log_keep = jnp.where(mask, neg_log_keep, 0.0)
    hi = _bits_op(neg_log_keep, jnp.bitwise_and, BF16_BITS)
    lo = neg_log_keep - hi
    hilo = jnp.concatenate([hi.astype(BF16), lo.astype(BF16)], axis=1)
    return log_beta, neg_log_keep[:, 0:1], hilo


def _sb_log_weights(log_beta, first_col, after, carry, mask):
    x = log_beta - after - carry
    if mask is not None:
        x = jnp.where(mask, x, NEG_BIG)
    return x, carry + (after[:, 0:1] + first_col)


def _sb_first_half(scores_fn, upper2, carry, mask):
    log_beta, first_col, hilo = _sb_terms(scores_fn(), mask)
    after = jnp.dot(hilo, upper2, preferred_element_type=F32)
    return _sb_log_weights(log_beta, first_col, after, carry, mask)


def _sb_weighted_values(x, v):
    return jnp.dot(jnp.exp2(x).astype(BF16), v, preferred_element_type=F32)


def _strict_lower_mask(nq, nk, period):
    r = lax.broadcasted_iota(jnp.int32, (nq, nk), 0) % period
    c = lax.broadcasted_iota(jnp.int32, (nq, nk), 1)
    return c < r


def _first_keys_mask(nq, nk, n_valid):
    return lax.broadcasted_iota(jnp.int32, (nq, nk), 1) < n_valid


def _attn_prompt_kernel(q_ref, k_ref, v_ref, km_ref, vm_ref, u_ref, us_ref, o_ref,
                        acc_ref, carry_ref, x_ref):
    qi = pl.program_id(2)
    blk = ATT_BLOCK
    nh = ATT_PROMPT_HEADS
    cols = [slice(h * HEAD_DIM, (h + 1) * HEAD_DIM) for h in range(nh)]

    def block_rows(j):
        return pl.ds(pl.multiple_of(j * blk, blk), blk)

    def step(v_pending, pending_keys, k_next, next_rows, upper2_ref, mask):
        next_keys = upper2_ref.shape[1]
        scores, mid = {}, {}
        for t in range(nh + 2):
            if t < nh:
                scores[t] = _scores(q_ref[:, cols[t]], k_next[next_rows, cols[t]])
            h = t - 1
            if 0 <= h < nh:
                acc_ref[h] += _sb_weighted_values(x_ref[h, :, :pending_keys], v_pending(h))
                log_beta, first_col, hilo = _sb_terms(scores.pop(h), mask)
                mid[h] = (log_beta, first_col,
                          jnp.dot(hilo, upper2_ref[...], preferred_element_type=F32))
            h = t - 2
            if 0 <= h < nh:
                x, carry = _sb_log_weights(*mid.pop(h), carry_ref[h], mask)
                x_ref[h, :, :next_keys] = x
                carry_ref[h] = carry

    diag_mask = _strict_lower_mask(blk, blk, blk)
    for h in range(nh):
        acc_ref[h] = jnp.zeros((blk, HEAD_DIM), F32)
        x, carry = _sb_first_half(
            lambda: _scores(q_ref[:, cols[h]], k_ref[block_rows(qi), cols[h]]),
            u_ref[...], jnp.zeros((blk, 1), F32), diag_mask)
        x_ref[h] = x
        carry_ref[h] = carry

    def earlier(p):
        step(lambda h: v_ref[block_rows(p), cols[h]], blk, k_ref, block_rows(p - 1), u_ref, None)

    odd = qi % 2
    pl.when(odd == 1)(lambda: earlier(qi))
    first_pending = qi - odd

    def body(t, carry):
        earlier(first_pending - 2 * t)
        earlier(first_pending - 2 * t - 1)
        return carry

    lax.fori_loop(0, first_pending // 2, body, 0)

    step(lambda h: v_ref[block_rows(0), cols[h]], blk, km_ref, slice(None), us_ref,
         _first_keys_mask(blk, SMALL_BLOCK, N_META))
    for h in range(nh):
        out = _sb_weighted_values(x_ref[h, :, :SMALL_BLOCK], vm_ref[:, cols[h]])
        o_ref[:, cols[h]] = (acc_ref[h] + out).astype(o_ref.dtype)


def _attn_prompt(q, kv, upper2, upper2_small):
    blk = ATT_BLOCK
    tiles = SEQ // blk
    hw = ATT_PROMPT_HEADS * HEAD_DIM
    hb = N_HEADS // ATT_PROMPT_HEADS
    meta_blk = META_ROW0 // SMALL_BLOCK
    const = lambda a: pl.BlockSpec(a.shape, lambda b, h, i: (0, 0))
    return pl.pallas_call(
        _attn_prompt_kernel,
        out_shape=jax.ShapeDtypeStruct((N_PROMPT, D_MODEL), BF16),
        grid=(BATCH, hb, tiles),
        in_specs=[
            pl.BlockSpec((blk, hw), lambda b, h, i: (b * tiles + i, h)),
            pl.BlockSpec((SEQ, hw), lambda b, h, i: (b, h)),
            pl.BlockSpec((SEQ, hw), lambda b, h, i: (b, hb + h)),
            pl.BlockSpec((SMALL_BLOCK, hw), lambda b, h, i: (meta_blk, h)),
            pl.BlockSpec((SMALL_BLOCK, hw), lambda b, h, i: (meta_blk, hb + h)),
            const(upper2), const(upper2_small)],
        out_specs=pl.BlockSpec((blk, hw), lambda b, h, i: (b * tiles + i, h)),
        scratch_shapes=[pltpu.VMEM((ATT_PROMPT_HEADS, blk, HEAD_DIM), F32),
                        pltpu.VMEM((ATT_PROMPT_HEADS, blk, 1), F32),
                        pltpu.VMEM((ATT_PROMPT_HEADS, blk, blk), F32)],
        compiler_params=_params(3),
        name="attn_prompt",
    )(q, kv, kv, kv, kv, upper2, upper2_small)


def _head_rows(ref3, start, size, h):
    keys, nh, hd = ref3.shape
    ref2 = ref3.reshape(keys * nh, hd)
    return ref2[pl.ds(start * nh + h, size, stride=nh), :]


def _attn_sample_kernel(q_ref, kn_ref, vn_ref, kc_ref, vc_ref, km_ref, vm_ref, u_ref, us_ref,
                        o_ref):
    blk = ATT_BLOCK
    nh = ATT_SAMPLE_HEADS
    nq = nh * DEC_SEQ
    cols = [slice(h * HEAD_DIM, (h + 1) * HEAD_DIM) for h in range(nh)]
    qrows = [slice(h * DEC_SEQ, (h + 1) * DEC_SEQ) for h in range(nh)]
    fill = jnp.zeros((SMALL_BLOCK - DEC_SEQ, HEAD_DIM), BF16)

    def block(k_of, v_of, u, mask):
        scores = lambda: jnp.concatenate(
            [_scores(q_ref[:, cols[h]], k_of(h)) for h in range(nh)], axis=0)
        return scores, v_of, u, mask

    def cached(ref3, c):
        return lambda h: _head_rows(ref3, c * blk, blk, h).astype(BF16)

    blocks = [block(lambda h: jnp.concatenate([kn_ref[:, cols[h]], fill], axis=0),
                    lambda h: jnp.concatenate([vn_ref[:, cols[h]], fill], axis=0),
                    us_ref, _strict_lower_mask(nq, SMALL_BLOCK, DEC_SEQ))]
    for c in reversed(range(PAST_LEN // blk)):
        blocks.append(block(cached(kc_ref, c), cached(vc_ref, c), u_ref, None))
    blocks.append(block(lambda h: km_ref[:, cols[h]], lambda h: vm_ref[:, cols[h]],
                        us_ref, _first_keys_mask(nq, SMALL_BLOCK, N_META)))

    acc = jnp.zeros((nq, HEAD_DIM), F32)
    pending, carry = _sb_first_half(blocks[0][0], blocks[0][2][...], jnp.zeros((nq, 1), F32),
                                    blocks[0][3])
    for b in range(len(blocks)):
        nxt = None
        if b + 1 < len(blocks):
            nxt, carry = _sb_first_half(blocks[b + 1][0], blocks[b + 1][2][...], carry,
                                        blocks[b + 1][3])
        a = jnp.exp2(pending).astype(BF16)
        acc = acc + jnp.concatenate(
            [jnp.dot(a[qrows[h]], blocks[b][1](h), preferred_element_type=F32)
             for h in range(nh)], axis=0)
        pending = nxt
    for h in range(nh):
        o_ref[:, cols[h]] = acc[qrows[h]].astype(o_ref.dtype)


def _attn_sample(q, kv, cache_k, cache_v, upper2, upper2_small):
    nh = ATT_SAMPLE_HEADS
    hw = nh * HEAD_DIM
    hb = N_HEADS // nh
    row_blk0 = SAMPLE_ROW0 // DEC_SEQ
    meta_blk = META_ROW0 // SMALL_BLOCK
    new = lambda off: pl.BlockSpec((DEC_SEQ, hw), lambda s, h: (row_blk0 + s, off + h))
    meta = lambda off: pl.BlockSpec((SMALL_BLOCK, hw), lambda s, h: (meta_blk, off + h))
    cache = pl.BlockSpec((None, PAST_LEN, nh, HEAD_DIM), lambda s, h: (s, 0, h, 0))
    const = lambda a: pl.BlockSpec(a.shape, lambda s, h: (0, 0))
    return pl.pallas_call(
        _attn_sample_kernel,
        out_shape=jax.ShapeDtypeStruct((N_SAMPLE, D_MODEL), BF16),
        grid=(DEC_BATCH, hb),
        in_specs=[new(0), new(0), new(hb), cache, cache, meta(0), meta(hb),
                  const(upper2), const(upper2_small)],
        out_specs=pl.BlockSpec((DEC_SEQ, hw), lambda s, h: (s, h)),
        compiler_params=_params(2),
        name="attn_sample",
    )(q, kv, kv, cache_k, cache_v, kv, kv, upper2, upper2_small)


def kernel(x_prompt, x_sample, state_conv, cache_k, cache_v, meta, a_w_pw1, a_b_pw1, a_w_dw, a_b_dw, a_ln_g, a_ln_b, a_w_pw2, a_b_pw2, w_kv, w_q, w_o, ln_mix_g, ln_mix_b, ln_ffn_g, ln_ffn_b, w_gate, w_up, w_down):
    d = D_MODEL
    n_pad = N_ROWS - META_ROW0 - N_META
    zero_vec = jnp.zeros((1, d), F32)
    xp = x_prompt.reshape(N_PROMPT, d)
    x_tail = jnp.concatenate([x_sample.reshape(N_SAMPLE, d), meta, jnp.zeros((n_pad, d), F32)],
                             axis=0)
    tail_pad = jnp.zeros((N_TAIL - N_SAMPLE, d), BF16)

    def ffn(r_main, r_tail, xb, l):
        h = _swiglu(xb, w_gate, w_up, l)
        y = _matmul_stream(h, w_down, l, F32, DOWN_COL_TILE, "ffn_down")
        return r_main, r_tail, y, zero_vec, ln_ffn_g[l][None], ln_ffn_b[l][None]

    g = _conv_glu((xp.astype(BF16), x_tail.astype(BF16)), a_w_pw1[0], a_b_pw1[0][None])
    g_sample = g[SAMPLE_ROW0:META_ROW0].reshape(DEC_BATCH, DEC_SEQ, d)
    g_meta = g[META_ROW0:META_ROW0 + N_META]
    first = jnp.concatenate([jnp.zeros((CONV_HALO - N_META, d), F32), g_meta], axis=0)
    conv_args = (a_w_dw[0], a_b_dw[0][None], a_ln_g[0][None], a_ln_b[0][None])
    c_prompt = _conv_prompt(g, first, *conv_args)
    hist = jnp.concatenate([state_conv[0], jnp.zeros((1, CONV_W - 1, d), F32)], axis=0)
    new = jnp.concatenate([g_sample, g_meta[None]], axis=0)
    win = jnp.concatenate([jnp.zeros((DEC_BATCH + 1, CONV_LEAD, d), F32), hist, new], axis=1)
    c_tail = jnp.concatenate([_conv_streams(win, *conv_args), jnp.zeros((n_pad, d), BF16)],
                             axis=0)
    (m,) = _matmul((c_prompt, c_tail), a_w_pw2[0].astype(BF16), [F32], WIDE_COL_TILE,
                   "conv_pw2")
    x1, x1b = _add_ln(xp, x_tail, m, a_b_pw2[0][None], ln_mix_g[0][None], ln_mix_b[0][None],
                      [F32, BF16], "mix_ln")
    x2, x2b = _add_ln(*ffn(x1, x1[N_PROMPT:], x1b, 0), [F32, BF16], "ffn_ln")

    kvb, k_s, v_s, k_p, v_p = _kv_proj(x2b, w_kv.astype(BF16))
    (qb,) = _matmul(x2b, w_q[0].astype(BF16), [BF16], WIDE_COL_TILE, "q_proj",
                    scale=SCORE_SCALE)
    upper2 = _suffix_sum_matrix(ATT_BLOCK)
    upper2_small = _suffix_sum_matrix(SMALL_BLOCK)
    o_prompt = _attn_prompt(qb, kvb, upper2, upper2_small)
    o_sample = _attn_sample(qb, kvb, cache_k, cache_v, upper2, upper2_small)
    o_tail = jnp.concatenate([o_sample, tail_pad], axis=0)
    (m,) = _matmul((o_prompt, o_tail), w_o[0].astype(BF16), [F32], WIDE_COL_TILE, "o_proj")
    x3, x3b = _add_ln(x2, x2[N_PROMPT:], m, zero_vec, ln_mix_g[1][None], ln_mix_b[1][None],
                      [F32, BF16], "mix_ln")
    _, _, y, bias, ln_g, ln_b = ffn(x3, None, x3b, 1)
    y_prompt, y_sample = _final_ln(x3, y, bias, ln_g, ln_b)

    y_prompt = y_prompt.reshape(BATCH, SEQ, d)
    y_sample = y_sample.reshape(DEC_BATCH, DEC_SEQ, d)
    n_ctx = CONV_W - 1
    state_conv_prompt = jnp.stack(
        [g[(b + 1) * SEQ - n_ctx:(b + 1) * SEQ] for b in range(BATCH)], axis=0)[None]
    state_conv_sample = jnp.concatenate([state_conv[0][:, DEC_SEQ:], g_sample], axis=1)[None]

    sample_shape = (DEC_BATCH, DEC_SEQ, N_HEADS, HEAD_DIM)
    return (y_prompt, y_sample, state_conv_prompt, state_conv_sample,
            k_p, v_p, k_s.reshape(sample_shape), v_s.reshape(sample_shape))
```

```python
import functools

import jax
import jax.numpy as jnp
from jax import lax
from jax.experimental import pallas as pl
from jax.experimental.pallas import tpu as pltpu

D_MODEL = 4096
BATCH = 2
SEQ = 4096
DEC_BATCH = 16
DEC_SEQ = 16
PAST_LEN = 1024
N_META = 16
CONV_W = 31
HEAD_DIM = 128
D_FF = 11008
DEPTH = 2
LN_EPS = 1e-5

N_HEADS = D_MODEL // HEAD_DIM
ALPHA = (2.0 * DEPTH) ** 0.25
SCALE = HEAD_DIM ** -0.5
LOG2E = 1.4426950408889634
SCORE_SCALE = SCALE * LOG2E

LANES = 128
SUBLANES = 8
VMEM_LIMIT = 56 * 1024 * 1024
ROW_TILE = 512
STREAM_PIECE = 256
DOWN_COL_TILE = 512
GU_COL_TILE = 1024
GU_CHUNK = 256
WIDE_COL_TILE = min(2048, D_MODEL)
VMEM_LIMIT_PW1 = 60 * 1024 * 1024
LN_TILE = 256
LN_ROWS = 16
CONV_TILE = 256
CONV_ROWS = 64
CONV_COLS = 512
CONV_HALO = 32
ATT_BLOCK = 256
SMALL_BLOCK = 128
ATT_PROMPT_HEADS = 8
ATT_SAMPLE_HEADS = SUBLANES

N_PROMPT = BATCH * SEQ
N_SAMPLE = DEC_BATCH * DEC_SEQ
SAMPLE_ROW0 = N_PROMPT
META_ROW0 = N_PROMPT + N_SAMPLE
N_ROWS = -(-(META_ROW0 + N_META) // ROW_TILE) * ROW_TILE
N_TAIL = N_ROWS - N_PROMPT
CONV_LEAD = CONV_HALO - (CONV_W - 1)
SIGN_BIT = 0x80000000
BF16_BITS = 0xFFFF0000
NEG_BIG = -1e30

assert N_PROMPT % ROW_TILE == 0 and N_TAIL % ROW_TILE == 0
assert N_PROMPT % LN_TILE == 0 and N_SAMPLE == LN_TILE
assert META_ROW0 % SMALL_BLOCK == 0 and SEQ % ATT_BLOCK == 0 and PAST_LEN % ATT_BLOCK == 0

F32 = jnp.float32
BF16 = jnp.bfloat16


def _params(n_axes, vmem_limit=VMEM_LIMIT):
    return pltpu.CompilerParams(
        dimension_semantics=("arbitrary",) * n_axes, vmem_limit_bytes=vmem_limit)


def _stream_weights(srcs, n_cols_total, w16_ref, stage_ref, sem):
    j, i = pl.program_id(0), pl.program_id(1)
    n_tiles, n_i = pl.num_programs(0), pl.num_programs(1)
    n_w, k, bn = w16_ref.shape[1:]
    chunk = stage_ref.shape[2]
    n_chunks = k // chunk
    n_pieces = bn // STREAM_PIECE
    step = j * n_i + i

    def copies(tile, c, parity):
        out = []
        for w in range(n_w):
            for p in range(n_pieces):
                col0 = jnp.minimum(tile * bn + p * STREAM_PIECE, n_cols_total - STREAM_PIECE)
                src = srcs[w](pl.ds(c * chunk, chunk), pl.ds(col0, STREAM_PIECE))
                dst = stage_ref.at[parity, w, :, pl.ds(p * STREAM_PIECE, STREAM_PIECE)]
                out.append(pltpu.make_async_copy(src, dst, sem.at[parity, w, p]))
        return out

    def cast(tile, c, parity):
        rows = pl.ds(pl.multiple_of(c * chunk, 16), chunk)
        for w in range(n_w):
            w16_ref[tile % 2, w, rows, :] = stage_ref[parity, w].astype(BF16)

    def wanted(at_j, at_i):
        return jnp.minimum(at_j + 1, n_tiles - 1), jnp.minimum(at_i, n_chunks - 1)

    @pl.when(step == 0)
    def _():
        def body(c, carry):
            for cp in copies(0, c, c % 2):
                cp.start()
            for cp in copies(0, c, c % 2):
                cp.wait()
            cast(0, c, c % 2)
            return carry

        lax.fori_loop(0, n_chunks, body, 0)
        for cp in copies(*wanted(0, 0), 0):
            cp.start()

    tile, c = wanted(j, i)
    for cp in copies(tile, c, step % 2):
        cp.wait()
    wrap = i + 1 == n_i
    next_copies = copies(*wanted(jnp.where(wrap, j + 1, j), jnp.where(wrap, 0, i + 1)),
                         (step + 1) % 2)
    for cp in next_copies:
        cp.start()
    cast(tile, c, step % 2)

    def drain():
        for cp in next_copies:
            cp.wait()

    return lambda: pl.when(step + 1 == n_tiles * n_i)(drain)


def _stream_scratch(n_w, k, bn, chunk):
    assert k % chunk == 0 and chunk % 16 == 0 and bn % STREAM_PIECE == 0
    return [pltpu.VMEM((2, n_w, k, bn), BF16), pltpu.VMEM((2, n_w, chunk, bn), F32),
            pltpu.SemaphoreType.DMA((2, n_w, bn // STREAM_PIECE))]


def _row_specs(x, bm):
    if not isinstance(x, tuple):
        return [x], [pl.BlockSpec((bm, x.shape[1]), lambda j, i: (i, 0))], None
    main, tail = x
    n_main = main.shape[0] // bm
    k = main.shape[1]
    return ([main, tail],
            [pl.BlockSpec((bm, k), lambda j, i: (jnp.minimum(i, n_main - 1), 0)),
             pl.BlockSpec((bm, k), lambda j, i: (jnp.maximum(i - n_main, 0), 0),
                          pipeline_mode=pl.Buffered(1))],
            n_main)


def _n_rows(x):
    return sum(a.shape[0] for a in x) if isinstance(x, tuple) else x.shape[0]


def _on_row_source(n_main, x_refs, body):
    if n_main is None:
        body(x_refs[0])
        return
    i = pl.program_id(1)
    pl.when(i < n_main)(lambda: body(x_refs[0]))
    pl.when(i >= n_main)(lambda: body(x_refs[1]))


def _mm_kernel(n_main, scale, *refs):
    n_x = 1 if n_main is None else 2
    x_refs, (w_ref, *o_refs) = refs[:n_x], refs[n_x:]

    def body(x_ref):
        acc = jnp.dot(x_ref[...], w_ref[...], preferred_element_type=F32)
        if scale is not None:
            acc = acc * scale
        for o_ref in o_refs:
            o_ref[...] = acc.astype(o_ref.dtype)

    _on_row_source(n_main, x_refs, body)


def _matmul(x, w, out_dtypes, bn, name, layer=None, scale=None):
    arrays, x_specs, n_main = _row_specs(x, ROW_TILE)
    m, bm = _n_rows(x), ROW_TILE
    k, n = w.shape[-2:]
    if layer is None:
        w_spec = pl.BlockSpec((k, bn), lambda j, i: (0, j))
    else:
        w_spec = pl.BlockSpec((None, k, bn), lambda j, i: (layer, 0, j))
    return pl.pallas_call(
        functools.partial(_mm_kernel, n_main, scale),
        out_shape=[jax.ShapeDtypeStruct((m, n), dt) for dt in out_dtypes],
        grid=(n // bn, m // bm),
        in_specs=x_specs + [w_spec],
        out_specs=[pl.BlockSpec((bm, bn), lambda j, i: (i, j)) for _ in out_dtypes],
        compiler_params=_params(2),
        name=name,
    )(*arrays, w)


def _mm_stream_kernel(layer, n_cols_total, x_ref, w_hbm, o_ref, w16_ref, stage_ref, sem):
    finish_stream = _stream_weights([lambda r, c: w_hbm.at[layer, r, c]], n_cols_total,
                                 w16_ref, stage_ref, sem)
    j = pl.program_id(0)
    o_ref[...] = jnp.dot(x_ref[...], w16_ref[j % 2, 0],
                         preferred_element_type=F32).astype(o_ref.dtype)
    finish_stream()


def _matmul_stream(x, w, layer, out_dtype, bn, name):
    m, k = x.shape
    n = w.shape[2]
    bm = ROW_TILE
    n_chunks = max(c for c in range(1, m // bm + 1) if k % (16 * c) == 0)
    return pl.pallas_call(
        functools.partial(_mm_stream_kernel, layer, n),
        out_shape=jax.ShapeDtypeStruct((m, n), out_dtype),
        grid=(n // bn, m // bm),
        in_specs=[pl.BlockSpec((bm, k), lambda j, i: (i, 0)),
                  pl.BlockSpec(memory_space=pl.ANY)],
        out_specs=pl.BlockSpec((bm, bn), lambda j, i: (i, j)),
        scratch_shapes=_stream_scratch(1, k, bn, k // n_chunks),
        compiler_params=_params(2),
        name=name,
    )(x, w)


def _glu_kernel(n_main, n_half, *refs):
    n_x = 1 if n_main is None else 2
    x_refs, (w_hbm, ba_ref, bb_ref, o_ref, w16_ref, stage_ref, sem) = refs[:n_x], refs[n_x:]
    srcs = [lambda r, c: w_hbm.at[r, c],
            lambda r, c: w_hbm.at[r, pl.ds(c.start + n_half, c.size)]]
    finish_stream = _stream_weights(srcs, n_half, w16_ref, stage_ref, sem)
    slot = pl.program_id(0) % 2

    def body(x_ref):
        x = x_ref[...]
        a = jnp.dot(x, w16_ref[slot, 0], preferred_element_type=F32) + ba_ref[...]
        b = jnp.dot(x, w16_ref[slot, 1], preferred_element_type=F32) + bb_ref[...]
        o_ref[...] = (a * jax.nn.sigmoid(b)).astype(o_ref.dtype)

    _on_row_source(n_main, x_refs, body)
    finish_stream()


def _swiglu_kernel(layer, x_ref, wg_hbm, wu_hbm, o_ref, w16_ref, stage_ref, sem):
    srcs = [lambda r, c: wg_hbm.at[layer, r, c], lambda r, c: wu_hbm.at[layer, r, c]]
    finish_stream = _stream_weights(srcs, D_FF, w16_ref, stage_ref, sem)
    slot = pl.program_id(0) % 2
    x = x_ref[...]
    g = jnp.dot(x, w16_ref[slot, 0], preferred_element_type=F32)
    u = jnp.dot(x, w16_ref[slot, 1], preferred_element_type=F32)
    o_ref[...] = (g * jax.nn.sigmoid(g) * u).astype(o_ref.dtype)
    finish_stream()


def _conv_glu(x, w1, b1):
    arrays, x_specs, n_main = _row_specs(x, ROW_TILE)
    m, bm, bn = _n_rows(x), ROW_TILE, GU_COL_TILE
    k = w1.shape[0]
    n = w1.shape[1] // 2
    nb = n // bn
    assert k // GU_CHUNK <= m // bm
    return pl.pallas_call(
        functools.partial(_glu_kernel, n_main, n),
        out_shape=jax.ShapeDtypeStruct((m, n), F32),
        grid=(nb, m // bm),
        in_specs=x_specs + [
            pl.BlockSpec(memory_space=pl.ANY),
            pl.BlockSpec((1, bn), lambda j, i: (0, j)),
            pl.BlockSpec((1, bn), lambda j, i: (0, j + nb))],
        out_specs=pl.BlockSpec((bm, bn), lambda j, i: (i, j)),
        scratch_shapes=_stream_scratch(2, k, bn, GU_CHUNK),
        compiler_params=_params(2, VMEM_LIMIT_PW1),
        name="conv_pw1_glu",
    )(*arrays, w1, b1, b1)


def _swiglu(x, wg, wu, layer):
    m, k = x.shape
    bm, bn = ROW_TILE, GU_COL_TILE
    assert k // GU_CHUNK <= m // bm
    hbm_spec = pl.BlockSpec(memory_space=pl.ANY)
    return pl.pallas_call(
        functools.partial(_swiglu_kernel, layer),
        out_shape=jax.ShapeDtypeStruct((m, D_FF), BF16),
        grid=(pl.cdiv(D_FF, bn), m // bm),
        in_specs=[pl.BlockSpec((bm, k), lambda j, i: (i, 0)), hbm_spec, hbm_spec],
        out_specs=pl.BlockSpec((bm, bn), lambda j, i: (i, j)),
        scratch_shapes=_stream_scratch(2, k, bn, GU_CHUNK),
        compiler_params=_params(2),
        name="ffn_gate_up",
    )(x, wg, wu)


KV_HEADS = SUBLANES
KV_COLS = KV_HEADS * HEAD_DIM
PROMPT_TILES = N_PROMPT // ROW_TILE
assert PROMPT_TILES % 2 == 0 and SEQ % ROW_TILE == 0 and D_MODEL % KV_COLS == 0


def _store_heads(dst_ref, acc):
    rows, nh, hd = dst_ref.shape
    flat = dst_ref.reshape(rows * nh, hd)
    for h in range(nh):
        flat[pl.ds(h, rows, stride=nh), :] = acc[:, h * hd:(h + 1) * hd]


def _kv_kernel(x_ref, w_ref, kvb_ref, ks_ref, vs_ref, kp_ref, vp_ref,
               stage0_ref, stage1_ref, meta_stage_ref, sem, meta_sem):
    j, i = pl.program_id(0), pl.program_id(1)
    n_kb = D_MODEL // KV_COLS
    tiles_per_stream = SEQ // ROW_TILE
    acc = jnp.dot(x_ref[...], w_ref[...], preferred_element_type=F32)
    kvb_ref[...] = acc.astype(BF16)
    h0 = (j % n_kb) * KV_HEADS

    def prompt_copy(dst_ref, stage_ref, slot, tile):
        b = tile // tiles_per_stream
        r0 = N_META + (tile % tiles_per_stream) * ROW_TILE
        dst = dst_ref.at[b, pl.ds(r0, ROW_TILE), pl.ds(h0, KV_HEADS), :]
        return pltpu.make_async_copy(stage_ref, dst, sem.at[slot])

    def emit(dst_ref, sample_ref):
        for slot, stage_ref in enumerate((stage0_ref, stage1_ref)):
            @pl.when((i < PROMPT_TILES) & (i % 2 == slot))
            def _():
                @pl.when(i >= 2)
                def _():
                    prompt_copy(dst_ref, stage_ref, slot, i - 2).wait()

                _store_heads(stage_ref, acc)
                prompt_copy(dst_ref, stage_ref, slot, i).start()

        @pl.when(i == PROMPT_TILES)
        def _():
            prompt_copy(dst_ref, stage0_ref, 0, PROMPT_TILES - 2).wait()
            prompt_copy(dst_ref, stage1_ref, 1, PROMPT_TILES - 1).wait()
            _store_heads(sample_ref, acc[:N_SAMPLE])
            _store_heads(meta_stage_ref, acc[N_SAMPLE:N_SAMPLE + N_META])
            copies = [pltpu.make_async_copy(
                meta_stage_ref, dst_ref.at[b, pl.ds(0, N_META), pl.ds(h0, KV_HEADS), :],
                meta_sem.at[b]) for b in range(BATCH)]
            for c in copies:
                c.start()
            for c in copies:
                c.wait()

    pl.when(j < n_kb)(lambda: emit(kp_ref, ks_ref))
    pl.when(j >= n_kb)(lambda: emit(vp_ref, vs_ref))


def _kv_proj(x, w_kv):
    m, k = x.shape
    bm, bn = ROW_TILE, KV_COLS
    n_kb = D_MODEL // bn
    assert m // bm == PROMPT_TILES + 1
    small = jax.ShapeDtypeStruct((N_SAMPLE, N_HEADS, HEAD_DIM), F32)
    big = jax.ShapeDtypeStruct((BATCH, N_META + SEQ, N_HEADS, HEAD_DIM), F32)
    return pl.pallas_call(
        _kv_kernel,
        out_shape=[jax.ShapeDtypeStruct((m, 2 * D_MODEL), BF16), small, small, big, big],
        grid=(2 * n_kb, m // bm),
        in_specs=[pl.BlockSpec((bm, k), lambda j, i: (i, 0)),
                  pl.BlockSpec((k, bn), lambda j, i: (0, j))],
        out_specs=[
            pl.BlockSpec((bm, bn), lambda j, i: (i, j)),
            pl.BlockSpec((N_SAMPLE, KV_HEADS, HEAD_DIM),
                         lambda j, i: (0, jnp.minimum(j, n_kb - 1), 0)),
            pl.BlockSpec((N_SAMPLE, KV_HEADS, HEAD_DIM),
                         lambda j, i: (0, jnp.maximum(j - n_kb, 0), 0)),
            pl.BlockSpec(memory_space=pl.ANY),
            pl.BlockSpec(memory_space=pl.ANY)],
        scratch_shapes=[pltpu.VMEM((bm, KV_HEADS, HEAD_DIM), F32),
                        pltpu.VMEM((bm, KV_HEADS, HEAD_DIM), F32),
                        pltpu.VMEM((N_META, KV_HEADS, HEAD_DIM), F32),
                        pltpu.SemaphoreType.DMA((2,)),
                        pltpu.SemaphoreType.DMA((BATCH,))],
        compiler_params=_params(2),
        name="kv_proj",
    )(x, w_kv)


def _layer_norm_rows(x, g, b):
    mu = jnp.mean(x, axis=-1, keepdims=True)
    xc = x - mu
    var = jnp.mean(xc * xc, axis=-1, keepdims=True)
    return xc * lax.rsqrt(var + LN_EPS) * g + b


def _add_ln_rows(r_ref, m_ref, bias_ref, g_ref, b_ref, o_refs):
    def body(c, carry):
        rows = pl.ds(pl.multiple_of(c * LN_ROWS, LN_ROWS), LN_ROWS)
        x = ALPHA * r_ref[rows, :] + (m_ref[rows, :] + bias_ref[...])
        y = _layer_norm_rows(x, g_ref[...], b_ref[...])
        for o_ref in o_refs:
            o_ref[rows, :] = y.astype(o_ref.dtype)
        return carry

    lax.fori_loop(0, r_ref.shape[0] // LN_ROWS, body, 0, unroll=2)


def _add_ln_kernel(n_main, r_main_ref, r_tail_ref, m_ref, bias_ref, g_ref, b_ref, *o_refs):
    i = pl.program_id(0)

    @pl.when(i < n_main)
    def _():
        _add_ln_rows(r_main_ref, m_ref, bias_ref, g_ref, b_ref, o_refs)

    @pl.when(i >= n_main)
    def _():
        _add_ln_rows(r_tail_ref, m_ref, bias_ref, g_ref, b_ref, o_refs)


def _add_ln(r_main, r_tail, m, bias, g, b, out_dtypes, name):
    rows, d = m.shape
    tm = LN_TILE
    n_main = N_PROMPT // tm
    row_spec = pl.BlockSpec((tm, d), lambda i: (i, 0))
    vec_spec = pl.BlockSpec((1, d), lambda i: (0, 0))
    return pl.pallas_call(
        functools.partial(_add_ln_kernel, n_main),
        out_shape=[jax.ShapeDtypeStruct((rows, d), dt) for dt in out_dtypes],
        grid=(rows // tm,),
        in_specs=[pl.BlockSpec((tm, d), lambda i: (jnp.minimum(i, n_main - 1), 0)),
                  pl.BlockSpec((tm, d), lambda i: (jnp.maximum(i - n_main, 0), 0)),
                  row_spec, vec_spec, vec_spec, vec_spec],
        out_specs=[row_spec for _ in out_dtypes],
        compiler_params=_params(1),
        name=name,
    )(r_main, r_tail, m, bias, g, b)


def _final_ln_kernel(n_main, r_ref, m_ref, bias_ref, g_ref, b_ref, o_main_ref, o_sample_ref):
    i = pl.program_id(0)

    @pl.when(i < n_main)
    def _():
        _add_ln_rows(r_ref, m_ref, bias_ref, g_ref, b_ref, [o_main_ref])

    @pl.when(i == n_main)
    def _():
        _add_ln_rows(r_ref, m_ref, bias_ref, g_ref, b_ref, [o_sample_ref])


def _final_ln(r, m, bias, g, b):
    d = m.shape[1]
    tm = LN_TILE
    n_main = N_PROMPT // tm
    row_spec = pl.BlockSpec((tm, d), lambda i: (i, 0))
    vec_spec = pl.BlockSpec((1, d), lambda i: (0, 0))
    return pl.pallas_call(
        functools.partial(_final_ln_kernel, n_main),
        out_shape=[jax.ShapeDtypeStruct((N_PROMPT, d), F32),
                   jax.ShapeDtypeStruct((N_SAMPLE, d), F32)],
        grid=(n_main + 1,),
        in_specs=[row_spec, row_spec, vec_spec, vec_spec, vec_spec],
        out_specs=[pl.BlockSpec((tm, d), lambda i: (jnp.minimum(i, n_main - 1), 0)),
                   pl.BlockSpec((tm, d), lambda i: (0, 0))],
        compiler_params=_params(1),
        name="final_ln",
    )(r, m, bias, g, b)


def _conv_ln_silu(win_ref, wdw_ref, bdw_ref, g_ref, b_ref, o_ref, c_ref, sh_ref, t_rows):
    d = win_ref.shape[1]
    rh = min(CONV_ROWS, t_rows)
    sh_rows = sh_ref.shape[1]

    def col_body(cc, carry):
        cols = pl.ds(pl.multiple_of(cc * CONV_COLS, CONV_COLS), CONV_COLS)
        for r in range(1, SUBLANES):
            sh_ref[r - 1] = win_ref[r:r + sh_rows, cols]
        for r0 in range(0, t_rows, rh):
            acc = jnp.zeros((rh, CONV_COLS), F32) + bdw_ref[:, cols]
            for k in range(CONV_W):
                a, r = divmod(CONV_LEAD + k, SUBLANES)
                lo = r0 + a * SUBLANES
                tap = win_ref[lo:lo + rh, cols] if r == 0 else sh_ref[r - 1, lo:lo + rh, :]
                acc = acc + wdw_ref[k:k + 1, cols] * tap
            c_ref[r0:r0 + rh, cols] = acc
        return carry

    lax.fori_loop(0, d // CONV_COLS, col_body, 0)

    def ln_body(c, carry):
        rows = pl.ds(pl.multiple_of(c * LN_ROWS, LN_ROWS), LN_ROWS)
        y = _layer_norm_rows(c_ref[rows, :], g_ref[...], b_ref[...])
        o_ref[rows, :] = (y * jax.nn.sigmoid(y)).astype(o_ref.dtype)
        return carry

    n_ln = t_rows // LN_ROWS
    lax.fori_loop(0, n_ln, ln_body, 0, unroll=2 if n_ln % 2 == 0 else 1)


def _conv_prompt_kernel(prev_ref, cur_ref, first_ref, wdw_ref, bdw_ref, g_ref, b_ref,
                        o_ref, win_ref, c_ref, sh_ref):
    t_rows = cur_ref.shape[0]
    i = pl.program_id(1)

    @pl.when(i == 0)
    def _():
        win_ref[0:CONV_HALO, :] = first_ref[...]

    @pl.when(i > 0)
    def _():
        win_ref[0:CONV_HALO, :] = prev_ref[...]

    win_ref[CONV_HALO:CONV_HALO + t_rows, :] = cur_ref[...]
    _conv_ln_silu(win_ref, wdw_ref, bdw_ref, g_ref, b_ref, o_ref, c_ref, sh_ref, t_rows)


def _conv_stream_kernel(win_ref, wdw_ref, bdw_ref, g_ref, b_ref, o_ref, c_ref, sh_ref):
    _conv_ln_silu(win_ref, wdw_ref, bdw_ref, g_ref, b_ref, o_ref, c_ref, sh_ref, o_ref.shape[0])


def _conv_scratch(t, d):
    sh_rows = CONV_HALO + t - SUBLANES
    return [pltpu.VMEM((t, d), F32), pltpu.VMEM((SUBLANES - 1, sh_rows, CONV_COLS), F32)]


def _conv_prompt(g, first, wdw, bdw, ln_g, ln_b):
    d = g.shape[1]
    t = CONV_TILE
    tiles = SEQ // t
    halo_per_tile = t // CONV_HALO
    vec = lambda r: pl.BlockSpec((r, d), lambda b, i: (0, 0))
    return pl.pallas_call(
        _conv_prompt_kernel,
        out_shape=jax.ShapeDtypeStruct((N_PROMPT, d), BF16),
        grid=(BATCH, tiles),
        in_specs=[
            pl.BlockSpec((CONV_HALO, d),
                         lambda b, i: (jnp.maximum((b * tiles + i) * halo_per_tile - 1, 0), 0)),
            pl.BlockSpec((t, d), lambda b, i: (b * tiles + i, 0)),
            vec(CONV_HALO), vec(CONV_W), vec(1), vec(1), vec(1)],
        out_specs=pl.BlockSpec((t, d), lambda b, i: (b * tiles + i, 0)),
        scratch_shapes=[pltpu.VMEM((CONV_HALO + t, d), F32)] + _conv_scratch(t, d),
        compiler_params=_params(2),
        name="conv_prompt",
    )(g, g, first, wdw, bdw, ln_g, ln_b)


def _conv_streams(win, wdw, bdw, ln_g, ln_b):
    s, rows, d = win.shape
    t = rows - CONV_HALO
    vec = lambda r: pl.BlockSpec((r, d), lambda i: (0, 0))
    return pl.pallas_call(
        _conv_stream_kernel,
        out_shape=jax.ShapeDtypeStruct((s * t, d), BF16),
        grid=(s,),
        in_specs=[pl.BlockSpec((None, rows, d), lambda i: (i, 0, 0)),
                  vec(CONV_W), vec(1), vec(1), vec(1)],
        out_specs=pl.BlockSpec((t, d), lambda i: (i, 0)),
        scratch_shapes=_conv_scratch(t, d),
        compiler_params=_params(1),
        name="conv_streams",
    )(win, wdw, bdw, ln_g, ln_b)


def _suffix_sum_matrix(n):
    r = lax.broadcasted_iota(jnp.int32, (2 * n, n), 0) % n
    c = lax.broadcasted_iota(jnp.int32, (2 * n, n), 1)
    return jnp.where(r > c, 1.0, 0.0).astype(BF16)


def _bits_op(x, op, bits):
    u = lax.bitcast_convert_type(x, jnp.uint32)
    return lax.bitcast_convert_type(op(u, jnp.uint32(bits)), F32)


def _scores(q, k):
    return lax.dot_general(q, k, (((1,), (1,)), ((), ())), preferred_element_type=F32)


def _sb_terms(z2, mask):
    neg_abs = _bits_op(z2, jnp.bitwise_or, SIGN_BIT)
    t2 = jnp.log(1.0 + jnp.exp2(neg_abs)) * LOG2E
    log_beta = jnp.minimum(z2, 0.0) - t2
    neg_log_keep = z2 - log_beta
    if mask is not None:
        neg_log_keep = jnp.where(mask, neg_log_keep, 0.0)
    hi = _bits_op(neg_log_keep, jnp.bitwise_and, BF16_BITS)
    lo = neg_log_keep - hi
    hilo = jnp.concatenate([hi.astype(BF16), lo.astype(BF16)], axis=1)
    return log_beta, neg_log_keep[:, 0:1], hilo


def _sb_log_weights(log_beta, first_col, after, carry, mask):
    x = log_beta - after - carry
    if mask is not None:
        x = jnp.where(mask, x, NEG_BIG)
    return x, carry + (after[:, 0:1] + first_col)


def _sb_first_half(scores_fn, upper2, carry, mask):
    log_beta, first_col, hilo = _sb_terms(scores_fn(), mask)
    after = jnp.dot(hilo, upper2, preferred_element_type=F32)
    return _sb_log_weights(log_beta, first_col, after, carry, mask)


def _sb_weighted_values(x, v):
    return jnp.dot(jnp.exp2(x).astype(BF16), v, preferred_element_type=F32)


def _strict_lower_mask(nq, nk, period):
    r = lax.broadcasted_iota(jnp.int32, (nq, nk), 0) % period
    c = lax.broadcasted_iota(jnp.int32, (nq, nk), 1)
    return c < r


def _first_keys_mask(nq, nk, n_valid):
    return lax.broadcasted_iota(jnp.int32, (nq, nk), 1) < n_valid


def _attn_prompt_kernel(q_ref, k_ref, v_ref, km_ref, vm_ref, u_ref, us_ref, o_ref,
                        acc_ref, carry_ref, x_ref):
    qi = pl.program_id(2)
    blk = ATT_BLOCK
    nh = ATT_PROMPT_HEADS
    cols = [slice(h * HEAD_DIM, (h + 1) * HEAD_DIM) for h in range(nh)]

    def block_rows(j):
        return pl.ds(pl.multiple_of(j * blk, blk), blk)

    def step(v_pending, pending_keys, k_next, next_rows, upper2_ref, mask):
        next_keys = upper2_ref.shape[1]
        scores, mid = {}, {}
        for t in range(nh + 2):
            if t < nh:
                scores[t] = _scores(q_ref[:, cols[t]], k_next[next_rows, cols[t]])
            h = t - 1
            if 0 <= h < nh:
                acc_ref[h] += _sb_weighted_values(x_ref[h, :, :pending_keys], v_pending(h))
                log_beta, first_col, hilo = _sb_terms(scores.pop(h), mask)
                mid[h] = (log_beta, first_col,
                          jnp.dot(hilo, upper2_ref[...], preferred_element_type=F32))
            h = t - 2
            if 0 <= h < nh:
                x, carry = _sb_log_weights(*mid.pop(h), carry_ref[h], mask)
                x_ref[h, :, :next_keys] = x
                carry_ref[h] = carry

    diag_mask = _strict_lower_mask(blk, blk, blk)
    for h in range(nh):
        acc_ref[h] = jnp.zeros((blk, HEAD_DIM), F32)
        x, carry = _sb_first_half(
            lambda: _scores(q_ref[:, cols[h]], k_ref[block_rows(qi), cols[h]]),
            u_ref[...], jnp.zeros((blk, 1), F32), diag_mask)
        x_ref[h] = x
        carry_ref[h] = carry

    def earlier(p):
        step(lambda h: v_ref[block_rows(p), cols[h]], blk, k_ref, block_rows(p - 1), u_ref, None)

    odd = qi % 2
    pl.when(odd == 1)(lambda: earlier(qi))
    first_pending = qi - odd

    def body(t, carry):
        earlier(first_pending - 2 * t)
        earlier(first_pending - 2 * t - 1)
        return carry

    lax.fori_loop(0, first_pending // 2, body, 0)

    step(lambda h: v_ref[block_rows(0), cols[h]], blk, km_ref, slice(None), us_ref,
         _first_keys_mask(blk, SMALL_BLOCK, N_META))
    for h in range(nh):
        out = _sb_weighted_values(x_ref[h, :, :SMALL_BLOCK], vm_ref[:, cols[h]])
        o_ref[:, cols[h]] = (acc_ref[h] + out).astype(o_ref.dtype)


def _attn_prompt(q, kv, upper2, upper2_small):
    blk = ATT_BLOCK
    tiles = SEQ // blk
    hw = ATT_PROMPT_HEADS * HEAD_DIM
    hb = N_HEADS // ATT_PROMPT_HEADS
    meta_blk = META_ROW0 // SMALL_BLOCK
    const = lambda a: pl.BlockSpec(a.shape, lambda b, h, i: (0, 0))
    return pl.pallas_call(
        _attn_prompt_kernel,
        out_shape=jax.ShapeDtypeStruct((N_PROMPT, D_MODEL), BF16),
        grid=(BATCH, hb, tiles),
        in_specs=[
            pl.BlockSpec((blk, hw), lambda b, h, i: (b * tiles + i, h)),
            pl.BlockSpec((SEQ, hw), lambda b, h, i: (b, h)),
            pl.BlockSpec((SEQ, hw), lambda b, h, i: (b, hb + h)),
            pl.BlockSpec((SMALL_BLOCK, hw), lambda b, h, i: (meta_blk, h)),
            pl.BlockSpec((SMALL_BLOCK, hw), lambda b, h, i: (meta_blk, hb + h)),
            const(upper2), const(upper2_small)],
        out_specs=pl.BlockSpec((blk, hw), lambda b, h, i: (b * tiles + i, h)),
        scratch_shapes=[pltpu.VMEM((ATT_PROMPT_HEADS, blk, HEAD_DIM), F32),
                        pltpu.VMEM((ATT_PROMPT_HEADS, blk, 1), F32),
                        pltpu.VMEM((ATT_PROMPT_HEADS, blk, blk), F32)],
        compiler_params=_params(3),
        name="attn_prompt",
    )(q, kv, kv, kv, kv, upper2, upper2_small)


def _head_rows(ref3, start, size, h):
    keys, nh, hd = ref3.shape
    ref2 = ref3.reshape(keys * nh, hd)
    return ref2[pl.ds(start * nh + h, size, stride=nh), :]


def _attn_sample_kernel(q_ref, kn_ref, vn_ref, kc_ref, vc_ref, km_ref, vm_ref, u_ref, us_ref,
                        o_ref):
    blk = ATT_BLOCK
    nh = ATT_SAMPLE_HEADS
    nq = nh * DEC_SEQ
    cols = [slice(h * HEAD_DIM, (h + 1) * HEAD_DIM) for h in range(nh)]
    qrows = [slice(h * DEC_SEQ, (h + 1) * DEC_SEQ) for h in range(nh)]
    fill = jnp.zeros((SMALL_BLOCK - DEC_SEQ, HEAD_DIM), BF16)

    def block(k_of, v_of, u, mask):
        scores = lambda: jnp.concatenate(
            [_scores(q_ref[:, cols[h]], k_of(h)) for h in range(nh)], axis=0)
        return scores, v_of, u, mask

    def cached(ref3, c):
        return lambda h: _head_rows(ref3, c * blk, blk, h).astype(BF16)

    blocks = [block(lambda h: jnp.concatenate([kn_ref[:, cols[h]], fill], axis=0),
                    lambda h: jnp.concatenate([vn_ref[:, cols[h]], fill], axis=0),
                    us_ref, _strict_lower_mask(nq, SMALL_BLOCK, DEC_SEQ))]
    for c in reversed(range(PAST_LEN // blk)):
        blocks.append(block(cached(kc_ref, c), cached(vc_ref, c), u_ref, None))
    blocks.append(block(lambda h: km_ref[:, cols[h]], lambda h: vm_ref[:, cols[h]],
                        us_ref, _first_keys_mask(nq, SMALL_BLOCK, N_META)))

    acc = jnp.zeros((nq, HEAD_DIM), F32)
    pending, carry = _sb_first_half(blocks[0][0], blocks[0][2][...], jnp.zeros((nq, 1), F32),
                                    blocks[0][3])
    for b in range(len(blocks)):
        nxt = None
        if b + 1 < len(blocks):
            nxt, carry = _sb_first_half(blocks[b + 1][0], blocks[b + 1][2][...], carry,
                                        blocks[b + 1][3])
        a = jnp.exp2(pending).astype(BF16)
        acc = acc + jnp.concatenate(
            [jnp.dot(a[qrows[h]], blocks[b][1](h), preferred_element_type=F32)
             for h in range(nh)], axis=0)
        pending = nxt
    for h in range(nh):
        o_ref[:, cols[h]] = acc[qrows[h]].astype(o_ref.dtype)


def _attn_sample(q, kv, cache_k, cache_v, upper2, upper2_small):
    nh = ATT_SAMPLE_HEADS
    hw = nh * HEAD_DIM
    hb = N_HEADS // nh
    row_blk0 = SAMPLE_ROW0 // DEC_SEQ
    meta_blk = META_ROW0 // SMALL_BLOCK
    new = lambda off: pl.BlockSpec((DEC_SEQ, hw), lambda s, h: (row_blk0 + s, off + h))
    meta = lambda off: pl.BlockSpec((SMALL_BLOCK, hw), lambda s, h: (meta_blk, off + h))
    cache = pl.BlockSpec((None, PAST_LEN, nh, HEAD_DIM), lambda s, h: (s, 0, h, 0))
    const = lambda a: pl.BlockSpec(a.shape, lambda s, h: (0, 0))
    return pl.pallas_call(
        _attn_sample_kernel,
        out_shape=jax.ShapeDtypeStruct((N_SAMPLE, D_MODEL), BF16),
        grid=(DEC_BATCH, hb),
        in_specs=[new(0), new(0), new(hb), cache, cache, meta(0), meta(hb),
                  const(upper2), const(upper2_small)],
        out_specs=pl.BlockSpec((DEC_SEQ, hw), lambda s, h: (s, h)),
        compiler_params=_params(2),
        name="attn_sample",
    )(q, kv, kv, cache_k, cache_v, kv, kv, upper2, upper2_small)


def kernel(x_prompt, x_sample, state_conv, cache_k, cache_v, meta, a_w_pw1, a_b_pw1, a_w_dw, a_b_dw, a_ln_g, a_ln_b, a_w_pw2, a_b_pw2, w_kv, w_q, w_o, ln_mix_g, ln_mix_b, ln_ffn_g, ln_ffn_b, w_gate, w_up, w_down):
    d = D_MODEL
    n_pad = N_ROWS - META_ROW0 - N_META
    zero_vec = jnp.zeros((1, d), F32)
    xp = x_prompt.reshape(N_PROMPT, d)
    x_tail = jnp.concatenate([x_sample.reshape(N_SAMPLE, d), meta, jnp.zeros((n_pad, d), F32)],
                             axis=0)
    tail_pad = jnp.zeros((N_TAIL - N_SAMPLE, d), BF16)

    def ffn(r_main, r_tail, xb, l):
        h = _swiglu(xb, w_gate, w_up, l)
        y = _matmul_stream(h, w_down, l, F32, DOWN_COL_TILE, "ffn_down")
        return r_main, r_tail, y, zero_vec, ln_ffn_g[l][None], ln_ffn_b[l][None]

    g = _conv_glu((xp.astype(BF16), x_tail.astype(BF16)), a_w_pw1[0], a_b_pw1[0][None])
    g_sample = g[SAMPLE_ROW0:META_ROW0].reshape(DEC_BATCH, DEC_SEQ, d)
    g_meta = g[META_ROW0:META_ROW0 + N_META]
    first = jnp.concatenate([jnp.zeros((CONV_HALO - N_META, d), F32), g_meta], axis=0)
    conv_args = (a_w_dw[0], a_b_dw[0][None], a_ln_g[0][None], a_ln_b[0][None])
    c_prompt = _conv_prompt(g, first, *conv_args)
    hist = jnp.concatenate([state_conv[0], jnp.zeros((1, CONV_W - 1, d), F32)], axis=0)
    new = jnp.concatenate([g_sample, g_meta[None]], axis=0)
    win = jnp.concatenate([jnp.zeros((DEC_BATCH + 1, CONV_LEAD, d), F32), hist, new], axis=1)
    c_tail = jnp.concatenate([_conv_streams(win, *conv_args), jnp.zeros((n_pad, d), BF16)],
                             axis=0)
    (m,) = _matmul((c_prompt, c_tail), a_w_pw2[0].astype(BF16), [F32], WIDE_COL_TILE,
                   "conv_pw2")
    x1, x1b = _add_ln(xp, x_tail, m, a_b_pw2[0][None], ln_mix_g[0][None], ln_mix_b[0][None],
                      [F32, BF16], "mix_ln")
    x2, x2b = _add_ln(*ffn(x1, x1[N_PROMPT:], x1b, 0), [F32, BF16], "ffn_ln")

    kvb, k_s, v_s, k_p, v_p = _kv_proj(x2b, w_kv.astype(BF16))
    (qb,) = _matmul(x2b, w_q[0].astype(BF16), [BF16], WIDE_COL_TILE, "q_proj",
                    scale=SCORE_SCALE)
    upper2 = _suffix_sum_matrix(ATT_BLOCK)
    upper2_small = _suffix_sum_matrix(SMALL_BLOCK)
    o_prompt = _attn_prompt(qb, kvb, upper2, upper2_small)
    o_sample = _attn_sample(qb, kvb, cache_k, cache_v, upper2, upper2_small)
    o_tail = jnp.concatenate([o_sample, tail_pad], axis=0)
    (m,) = _matmul((o_prompt, o_tail), w_o[0].astype(BF16), [F32], WIDE_COL_TILE, "o_proj")
    x3, x3b = _add_ln(x2, x2[N_PROMPT:], m, zero_vec, ln_mix_g[1][None], ln_mix_b[1][None],
                      [F32, BF16], "mix_ln")
    _, _, y, bias, ln_g, ln_b = ffn(x3, None, x3b, 1)
    y_prompt, y_sample = _final_ln(x3, y, bias, ln_g, ln_b)

    y_prompt = y_prompt.reshape(BATCH, SEQ, d)
    y_sample = y_sample.reshape(DEC_BATCH, DEC_SEQ, d)
    n_ctx = CONV_W - 1
    state_conv_prompt = jnp.stack(
        [g[(b + 1) * SEQ - n_ctx:(b + 1) * SEQ] for b in range(BATCH)], axis=0)[None]
    state_conv_sample = jnp.concatenate([state_conv[0][:, DEC_SEQ:], g_sample], axis=1)[None]

    sample_shape = (DEC_BATCH, DEC_SEQ, N_HEADS, HEAD_DIM)
    return (y_prompt, y_sample, state_conv_prompt, state_conv_sample,
            k_p, v_p, k_s.reshape(sample_shape), v_s.reshape(sample_shape))
```

```python
import functools

import jax
import jax.numpy as jnp
from jax import lax
from jax.experimental import pallas as pl
from jax.experimental.pallas import tpu as pltpu

D_MODEL = 4096
BATCH = 2
SEQ = 4096
DEC_BATCH = 16
DEC_SEQ = 16
PAST_LEN = 1024
N_META = 16
CONV_W = 31
HEAD_DIM = 128
D_FF = 11008
DEPTH = 2
LN_EPS = 1e-5

N_HEADS = D_MODEL // HEAD_DIM
ALPHA = (2.0 * DEPTH) ** 0.25
SCALE = HEAD_DIM ** -0.5
LOG2E = 1.4426950408889634
SCORE_SCALE = SCALE * LOG2E

LANES = 128
SUBLANES = 8
BF16_ROWS = 16
VMEM_LIMIT = 56 * 1024 * 1024
ROW_TILE = 512
STREAM_PIECE = 256
DOWN_COL_TILE = 512
GU_COL_TILE = 1024
GU_CHUNK = 256
WIDE_COL_TILE = min(2048, D_MODEL)
VMEM_LIMIT_PW1 = 60 * 1024 * 1024
LN_TILE = 256
LN_ROWS = 16
CONV_TILE = 256
CONV_ROWS = 64
CONV_COLS = 512
CONV_HALO = 32
ATT_BLOCK = 256
SMALL_BLOCK = 128
ATT_PROMPT_HEADS = 8
ATT_SAMPLE_HEADS = SUBLANES

N_PROMPT = BATCH * SEQ
N_SAMPLE = DEC_BATCH * DEC_SEQ
SAMPLE_ROW0 = N_PROMPT
META_ROW0 = N_PROMPT + N_SAMPLE
N_ROWS = -(-(META_ROW0 + N_META) // ROW_TILE) * ROW_TILE
N_TAIL = N_ROWS - N_PROMPT
CONV_LEAD = CONV_HALO - (CONV_W - 1)
SIGN_BIT = 0x80000000
BF16_BITS = 0xFFFF0000
NEG_BIG = -1e30

assert N_PROMPT % ROW_TILE == 0 and N_TAIL % ROW_TILE == 0
assert N_PROMPT % LN_TILE == 0 and N_SAMPLE == LN_TILE
assert META_ROW0 % SMALL_BLOCK == 0 and SEQ % ATT_BLOCK == 0 and PAST_LEN % ATT_BLOCK == 0

F32 = jnp.float32
BF16 = jnp.bfloat16


def _params(n_axes, vmem_limit=VMEM_LIMIT):
    return pltpu.CompilerParams(
        dimension_semantics=("arbitrary",) * n_axes, vmem_limit_bytes=vmem_limit)


def _stream_weights(srcs, n_cols_total, w16_ref, stage_ref, sem):
    j, i = pl.program_id(0), pl.program_id(1)
    n_tiles, n_i = pl.num_programs(0), pl.num_programs(1)
    n_w, k, bn = w16_ref.shape[1:]
    chunk = stage_ref.shape[2]
    n_chunks = k // chunk
    n_pieces = bn // STREAM_PIECE
    step = j * n_i + i

    def copies(tile, c, parity):
        out = []
        for w in range(n_w):
            for p in range(n_pieces):
                col0 = jnp.minimum(tile * bn + p * STREAM_PIECE, n_cols_total - STREAM_PIECE)
                src = srcs[w](pl.ds(c * chunk, chunk), pl.ds(col0, STREAM_PIECE))
                dst = stage_ref.at[parity, w, :, pl.ds(p * STREAM_PIECE, STREAM_PIECE)]
                out.append(pltpu.make_async_copy(src, dst, sem.at[parity, w, p]))
        return out

    def cast(tile, c, parity):
        rows = pl.ds(pl.multiple_of(c * chunk, BF16_ROWS), chunk)
        for w in range(n_w):
            w16_ref[tile % 2, w, rows, :] = stage_ref[parity, w].astype(BF16)

    def wanted(at_j, at_i):
        return jnp.minimum(at_j + 1, n_tiles - 1), jnp.minimum(at_i, n_chunks - 1)

    @pl.when(step == 0)
    def _():
        def body(c, carry):
            for cp in copies(0, c, c % 2):
                cp.start()
            for cp in copies(0, c, c % 2):
                cp.wait()
            cast(0, c, c % 2)
            return carry

        lax.fori_loop(0, n_chunks, body, 0)
        for cp in copies(*wanted(0, 0), 0):
            cp.start()

    tile, c = wanted(j, i)
    for cp in copies(tile, c, step % 2):
        cp.wait()
    wrap = i + 1 == n_i
    next_copies = copies(*wanted(jnp.where(wrap, j + 1, j), jnp.where(wrap, 0, i + 1)),
                         (step + 1) % 2)
    for cp in next_copies:
        cp.start()
    cast(tile, c, step % 2)

    def drain():
        for cp in next_copies:
            cp.wait()

    return lambda: pl.when(step + 1 == n_tiles * n_i)(drain)


def _stream_scratch(n_w, k, bn, chunk):
    assert k % chunk == 0 and chunk % BF16_ROWS == 0 and bn % STREAM_PIECE == 0
    return [pltpu.VMEM((2, n_w, k, bn), BF16), pltpu.VMEM((2, n_w, chunk, bn), F32),
            pltpu.SemaphoreType.DMA((2, n_w, bn // STREAM_PIECE))]


def _row_specs(x, bm):
    if not isinstance(x, tuple):
        return [x], [pl.BlockSpec((bm, x.shape[1]), lambda j, i: (i, 0))], None
    main, tail = x
    n_main = main.shape[0] // bm
    k = main.shape[1]
    return ([main, tail],
            [pl.BlockSpec((bm, k), lambda j, i: (jnp.minimum(i, n_main - 1), 0)),
             pl.BlockSpec((bm, k), lambda j, i: (jnp.maximum(i - n_main, 0), 0),
                          pipeline_mode=pl.Buffered(1))],
            n_main)


def _n_rows(x):
    return sum(a.shape[0] for a in x) if isinstance(x, tuple) else x.shape[0]


def _on_row_source(n_main, x_refs, body):
    if n_main is None:
        body(x_refs[0])
        return
    i = pl.program_id(1)
    pl.when(i < n_main)(lambda: body(x_refs[0]))
    pl.when(i >= n_main)(lambda: body(x_refs[1]))


def _mm_kernel(n_main, scale, *refs):
    n_x = 1 if n_main is None else 2
    x_refs, (w_ref, *o_refs) = refs[:n_x], refs[n_x:]

    def body(x_ref):
        acc = jnp.dot(x_ref[...], w_ref[...], preferred_element_type=F32)
        if scale is not None:
            acc = acc * scale
        for o_ref in o_refs:
            o_ref[...] = acc.astype(o_ref.dtype)

    _on_row_source(n_main, x_refs, body)


def _matmul(x, w, out_dtypes, bn, name, layer=None, scale=None):
    arrays, x_specs, n_main = _row_specs(x, ROW_TILE)
    m, bm = _n_rows(x), ROW_TILE
    k, n = w.shape[-2:]
    if layer is None:
        w_spec = pl.BlockSpec((k, bn), lambda j, i: (0, j))
    else:
        w_spec = pl.BlockSpec((None, k, bn), lambda j, i: (layer, 0, j))
    return pl.pallas_call(
        functools.partial(_mm_kernel, n_main, scale),
        out_shape=[jax.ShapeDtypeStruct((m, n), dt) for dt in out_dtypes],
        grid=(n // bn, m // bm),
        in_specs=x_specs + [w_spec],
        out_specs=[pl.BlockSpec((bm, bn), lambda j, i: (i, j)) for _ in out_dtypes],
        compiler_params=_params(2),
        name=name,
    )(*arrays, w)


def _mm_stream_kernel(layer, n_cols_total, scale, x_ref, w_hbm, o_ref, w16_ref, stage_ref,
                      sem):
    finish_stream = _stream_weights([lambda r, c: w_hbm.at[layer, r, c]], n_cols_total,
                                    w16_ref, stage_ref, sem)
    j = pl.program_id(0)
    acc = jnp.dot(x_ref[...], w16_ref[j % 2, 0], preferred_element_type=F32)
    if scale is not None:
        acc = acc * scale
    o_ref[...] = acc.astype(o_ref.dtype)
    finish_stream()


def _matmul_stream(x, w, layer, out_dtype, bn, name, scale=None):
    m, k = x.shape
    n = w.shape[2]
    bm = ROW_TILE
    n_chunks = max(c for c in range(1, m // bm + 1) if k % (BF16_ROWS * c) == 0)
    return pl.pallas_call(
        functools.partial(_mm_stream_kernel, layer, n, scale),
        out_shape=jax.ShapeDtypeStruct((m, n), out_dtype),
        grid=(n // bn, m // bm),
        in_specs=[pl.BlockSpec((bm, k), lambda j, i: (i, 0)),
                  pl.BlockSpec(memory_space=pl.ANY)],
        out_specs=pl.BlockSpec((bm, bn), lambda j, i: (i, j)),
        scratch_shapes=_stream_scratch(1, k, bn, k // n_chunks),
        compiler_params=_params(2),
        name=name,
    )(x, w)


def _glu_kernel(n_main, n_half, *refs):
    n_x = 1 if n_main is None else 2
    x_refs, (w_hbm, ba_ref, bb_ref, o_ref, w16_ref, stage_ref, sem) = refs[:n_x], refs[n_x:]
    srcs = [lambda r, c: w_hbm.at[r, c],
            lambda r, c: w_hbm.at[r, pl.ds(c.start + n_half, c.size)]]
    finish_stream = _stream_weights(srcs, n_half, w16_ref, stage_ref, sem)
    slot = pl.program_id(0) % 2

    def body(x_ref):
        x = x_ref[...]
        a = jnp.dot(x, w16_ref[slot, 0], preferred_element_type=F32) + ba_ref[...]
        b = jnp.dot(x, w16_ref[slot, 1], preferred_element_type=F32) + bb_ref[...]
        o_ref[...] = (a * jax.nn.sigmoid(b)).astype(o_ref.dtype)

    _on_row_source(n_main, x_refs, body)
    finish_stream()


def _swiglu_kernel(layer, x_ref, wg_hbm, wu_hbm, o_ref, w16_ref, stage_ref, sem):
    srcs = [lambda r, c: wg_hbm.at[layer, r, c], lambda r, c: wu_hbm.at[layer, r, c]]
    finish_stream = _stream_weights(srcs, D_FF, w16_ref, stage_ref, sem)
    slot = pl.program_id(0) % 2
    x = x_ref[...]
    g = jnp.dot(x, w16_ref[slot, 0], preferred_element_type=F32)
    u = jnp.dot(x, w16_ref[slot, 1], preferred_element_type=F32)
    o_ref[...] = (g * jax.nn.sigmoid(g) * u).astype(o_ref.dtype)
    finish_stream()


def _conv_glu(x, w1, b1):
    arrays, x_specs, n_main = _row_specs(x, ROW_TILE)
    m, bm, bn = _n_rows(x), ROW_TILE, GU_COL_TILE
    k = w1.shape[0]
    n = w1.shape[1] // 2
    nb = n // bn
    assert k // GU_CHUNK <= m // bm
    return pl.pallas_call(
        functools.partial(_glu_kernel, n_main, n),
        out_shape=jax.ShapeDtypeStruct((m, n), F32),
        grid=(nb, m // bm),
        in_specs=x_specs + [
            pl.BlockSpec(memory_space=pl.ANY),
            pl.BlockSpec((1, bn), lambda j, i: (0, j)),
            pl.BlockSpec((1, bn), lambda j, i: (0, j + nb))],
        out_specs=pl.BlockSpec((bm, bn), lambda j, i: (i, j)),
        scratch_shapes=_stream_scratch(2, k, bn, GU_CHUNK),
        compiler_params=_params(2, VMEM_LIMIT_PW1),
        name="conv_pw1_glu",
    )(*arrays, w1, b1, b1)


def _swiglu(x, wg, wu, layer):
    m, k = x.shape
    bm, bn = ROW_TILE, GU_COL_TILE
    assert k // GU_CHUNK <= m // bm
    hbm_spec = pl.BlockSpec(memory_space=pl.ANY)
    return pl.pallas_call(
        functools.partial(_swiglu_kernel, layer),
        out_shape=jax.ShapeDtypeStruct((m, D_FF), BF16),
        grid=(pl.cdiv(D_FF, bn), m // bm),
        in_specs=[pl.BlockSpec((bm, k), lambda j, i: (i, 0)), hbm_spec, hbm_spec],
        out_specs=pl.BlockSpec((bm, bn), lambda j, i: (i, j)),
        scratch_shapes=_stream_scratch(2, k, bn, GU_CHUNK),
        compiler_params=_params(2),
        name="ffn_gate_up",
    )(x, wg, wu)


KV_HEADS = SUBLANES
KV_COLS = KV_HEADS * HEAD_DIM
PROMPT_TILES = N_PROMPT // ROW_TILE
assert PROMPT_TILES % 2 == 0 and SEQ % ROW_TILE == 0 and D_MODEL % KV_COLS == 0


def _store_heads(dst_ref, acc):
    rows, nh, hd = dst_ref.shape
    flat = dst_ref.reshape(rows * nh, hd)
    for h in range(nh):
        flat[pl.ds(h, rows, stride=nh), :] = acc[:, h * hd:(h + 1) * hd]


def _kv_kernel(x_ref, w_hbm, kvb_ref, ks_ref, vs_ref, kp_ref, vp_ref,
               stage0_ref, stage1_ref, meta_stage_ref, sem, meta_sem, w16_ref, wstage_ref, wsem):
    j, i = pl.program_id(0), pl.program_id(1)
    n_kb = D_MODEL // KV_COLS
    tiles_per_stream = SEQ // ROW_TILE
    finish_stream = _stream_weights([lambda r, c: w_hbm.at[r, c]], 2 * D_MODEL,
                                    w16_ref, wstage_ref, wsem)
    acc = jnp.dot(x_ref[...], w16_ref[j % 2, 0], preferred_element_type=F32)
    kvb_ref[...] = acc.astype(BF16)
    h0 = (j % n_kb) * KV_HEADS

    def prompt_copy(dst_ref, stage_ref, slot, tile):
        b = tile // tiles_per_stream
        r0 = N_META + (tile % tiles_per_stream) * ROW_TILE
        dst = dst_ref.at[b, pl.ds(r0, ROW_TILE), pl.ds(h0, KV_HEADS), :]
        return pltpu.make_async_copy(stage_ref, dst, sem.at[slot])

    def emit(dst_ref, sample_ref):
        for slot, stage_ref in enumerate((stage0_ref, stage1_ref)):
            @pl.when((i < PROMPT_TILES) & (i % 2 == slot))
            def _():
                @pl.when(i >= 2)
                def _():
                    prompt_copy(dst_ref, stage_ref, slot, i - 2).wait()

                _store_heads(stage_ref, acc)
                prompt_copy(dst_ref, stage_ref, slot, i).start()

        @pl.when(i == PROMPT_TILES)
        def _():
            prompt_copy(dst_ref, stage0_ref, 0, PROMPT_TILES - 2).wait()
            prompt_copy(dst_ref, stage1_ref, 1, PROMPT_TILES - 1).wait()
            _store_heads(sample_ref, acc[:N_SAMPLE])
            _store_heads(meta_stage_ref, acc[N_SAMPLE:N_SAMPLE + N_META])
            copies = [pltpu.make_async_copy(
                meta_stage_ref, dst_ref.at[b, pl.ds(0, N_META), pl.ds(h0, KV_HEADS), :],
                meta_sem.at[b]) for b in range(BATCH)]
            for c in copies:
                c.start()
            for c in copies:
                c.wait()

    pl.when(j < n_kb)(lambda: emit(kp_ref, ks_ref))
    pl.when(j >= n_kb)(lambda: emit(vp_ref, vs_ref))
    finish_stream()


def _kv_proj(x, w_kv):
    m, k = x.shape
    bm, bn = ROW_TILE, KV_COLS
    n_kb = D_MODEL // bn
    assert m // bm == PROMPT_TILES + 1
    small = jax.ShapeDtypeStruct((N_SAMPLE, N_HEADS, HEAD_DIM), F32)
    big = jax.ShapeDtypeStruct((BATCH, N_META + SEQ, N_HEADS, HEAD_DIM), F32)
    return pl.pallas_call(
        _kv_kernel,
        out_shape=[jax.ShapeDtypeStruct((m, 2 * D_MODEL), BF16), small, small, big, big],
        grid=(2 * n_kb, m // bm),
        in_specs=[pl.BlockSpec((bm, k), lambda j, i: (i, 0)),
                  pl.BlockSpec(memory_space=pl.ANY)],
        out_specs=[
            pl.BlockSpec((bm, bn), lambda j, i: (i, j)),
            pl.BlockSpec((N_SAMPLE, KV_HEADS, HEAD_DIM),
                         lambda j, i: (0, jnp.minimum(j, n_kb - 1), 0)),
            pl.BlockSpec((N_SAMPLE, KV_HEADS, HEAD_DIM),
                         lambda j, i: (0, jnp.maximum(j - n_kb, 0), 0)),
            pl.BlockSpec(memory_space=pl.ANY),
            pl.BlockSpec(memory_space=pl.ANY)],
        scratch_shapes=[pltpu.VMEM((bm, KV_HEADS, HEAD_DIM), F32),
                        pltpu.VMEM((bm, KV_HEADS, HEAD_DIM), F32),
                        pltpu.VMEM((N_META, KV_HEADS, HEAD_DIM), F32),
                        pltpu.SemaphoreType.DMA((2,)),
                        pltpu.SemaphoreType.DMA((BATCH,))] + _stream_scratch(1, k, bn, GU_CHUNK),
        compiler_params=_params(2),
        name="kv_proj",
    )(x, w_kv)


def _layer_norm_rows(x, g, b):
    mu = jnp.mean(x, axis=-1, keepdims=True)
    xc = x - mu
    var = jnp.mean(xc * xc, axis=-1, keepdims=True)
    return xc * lax.rsqrt(var + LN_EPS) * g + b


def _add_ln_rows(r_ref, m_ref, bias_ref, g_ref, b_ref, o_refs):
    def body(c, carry):
        rows = pl.ds(pl.multiple_of(c * LN_ROWS, LN_ROWS), LN_ROWS)
        x = ALPHA * r_ref[rows, :] + (m_ref[rows, :] + bias_ref[...])
        y = _layer_norm_rows(x, g_ref[...], b_ref[...])
        for o_ref in o_refs:
            o_ref[rows, :] = y.astype(o_ref.dtype)
        return carry

    lax.fori_loop(0, r_ref.shape[0] // LN_ROWS, body, 0, unroll=2)


def _add_ln_kernel(n_main, r_main_ref, r_tail_ref, m_ref, bias_ref, g_ref, b_ref, *o_refs):
    i = pl.program_id(0)

    @pl.when(i < n_main)
    def _():
        _add_ln_rows(r_main_ref, m_ref, bias_ref, g_ref, b_ref, o_refs)

    @pl.when(i >= n_main)
    def _():
        _add_ln_rows(r_tail_ref, m_ref, bias_ref, g_ref, b_ref, o_refs)


def _add_ln(r_main, r_tail, m, bias, g, b, out_dtypes, name):
    rows, d = m.shape
    tm = LN_TILE
    n_main = N_PROMPT // tm
    row_spec = pl.BlockSpec((tm, d), lambda i: (i, 0))
    vec_spec = pl.BlockSpec((1, d), lambda i: (0, 0))
    return pl.pallas_call(
        functools.partial(_add_ln_kernel, n_main),
        out_shape=[jax.ShapeDtypeStruct((rows, d), dt) for dt in out_dtypes],
        grid=(rows // tm,),
        in_specs=[pl.BlockSpec((tm, d), lambda i: (jnp.minimum(i, n_main - 1), 0)),
                  pl.BlockSpec((tm, d), lambda i: (jnp.maximum(i - n_main, 0), 0)),
                  row_spec, vec_spec, vec_spec, vec_spec],
        out_specs=[row_spec for _ in out_dtypes],
        compiler_params=_params(1),
        name=name,
    )(r_main, r_tail, m, bias, g, b)


def _final_ln_kernel(n_main, r_ref, m_ref, bias_ref, g_ref, b_ref, o_main_ref, o_sample_ref):
    i = pl.program_id(0)

    @pl.when(i < n_main)
    def _():
        _add_ln_rows(r_ref, m_ref, bias_ref, g_ref, b_ref, [o_main_ref])

    @pl.when(i == n_main)
    def _():
        _add_ln_rows(r_ref, m_ref, bias_ref, g_ref, b_ref, [o_sample_ref])


def _final_ln(r, m, bias, g, b):
    d = m.shape[1]
    tm = LN_TILE
    n_main = N_PROMPT // tm
    row_spec = pl.BlockSpec((tm, d), lambda i: (i, 0))
    vec_spec = pl.BlockSpec((1, d), lambda i: (0, 0))
    return pl.pallas_call(
        functools.partial(_final_ln_kernel, n_main),
        out_shape=[jax.ShapeDtypeStruct((N_PROMPT, d), F32),
                   jax.ShapeDtypeStruct((N_SAMPLE, d), F32)],
        grid=(n_main + 1,),
        in_specs=[row_spec, row_spec, vec_spec, vec_spec, vec_spec],
        out_specs=[pl.BlockSpec((tm, d), lambda i: (jnp.minimum(i, n_main - 1), 0)),
                   pl.BlockSpec((tm, d), lambda i: (0, 0))],
        compiler_params=_params(1),
        name="final_ln",
    )(r, m, bias, g, b)


def _conv_ln_silu(win_ref, wdw_ref, bdw_ref, g_ref, b_ref, o_ref, c_ref, sh_ref, t_rows):
    d = win_ref.shape[1]
    rh = min(CONV_ROWS, t_rows)
    sh_rows = sh_ref.shape[1]

    def col_body(cc, carry):
        cols = pl.ds(pl.multiple_of(cc * CONV_COLS, CONV_COLS), CONV_COLS)
        for r in range(1, SUBLANES):
            sh_ref[r - 1] = win_ref[r:r + sh_rows, cols]
        for r0 in range(0, t_rows, rh):
            acc = jnp.zeros((rh, CONV_COLS), F32) + bdw_ref[:, cols]
            for k in range(CONV_W):
                a, r = divmod(CONV_LEAD + k, SUBLANES)
                lo = r0 + a * SUBLANES
                tap = win_ref[lo:lo + rh, cols] if r == 0 else sh_ref[r - 1, lo:lo + rh, :]
                acc = acc + wdw_ref[k:k + 1, cols] * tap
            c_ref[r0:r0 + rh, cols] = acc
        return carry

    lax.fori_loop(0, d // CONV_COLS, col_body, 0)

    def ln_body(c, carry):
        rows = pl.ds(pl.multiple_of(c * LN_ROWS, LN_ROWS), LN_ROWS)
        y = _layer_norm_rows(c_ref[rows, :], g_ref[...], b_ref[...])
        o_ref[rows, :] = (y * jax.nn.sigmoid(y)).astype(o_ref.dtype)
        return carry

    n_ln = t_rows // LN_ROWS
    lax.fori_loop(0, n_ln, ln_body, 0, unroll=2 if n_ln % 2 == 0 else 1)


def _conv_prompt_kernel(prev_ref, cur_ref, first_ref, wdw_ref, bdw_ref, g_ref, b_ref,
                        o_ref, win_ref, c_ref, sh_ref):
    t_rows = cur_ref.shape[0]
    i = pl.program_id(1)

    @pl.when(i == 0)
    def _():
        win_ref[0:CONV_HALO, :] = first_ref[...]

    @pl.when(i > 0)
    def _():
        win_ref[0:CONV_HALO, :] = prev_ref[...]

    win_ref[CONV_HALO:CONV_HALO + t_rows, :] = cur_ref[...]
    _conv_ln_silu(win_ref, wdw_ref, bdw_ref, g_ref, b_ref, o_ref, c_ref, sh_ref, t_rows)


def _conv_stream_kernel(win_ref, wdw_ref, bdw_ref, g_ref, b_ref, o_ref, c_ref, sh_ref):
    _conv_ln_silu(win_ref, wdw_ref, bdw_ref, g_ref, b_ref, o_ref, c_ref, sh_ref, o_ref.shape[0])


def _conv_scratch(t, d):
    sh_rows = CONV_HALO + t - SUBLANES
    return [pltpu.VMEM((t, d), F32), pltpu.VMEM((SUBLANES - 1, sh_rows, CONV_COLS), F32)]


def _conv_prompt(g, first, wdw, bdw, ln_g, ln_b):
    d = g.shape[1]
    t = CONV_TILE
    tiles = SEQ // t
    halo_per_tile = t // CONV_HALO
    vec = lambda r: pl.BlockSpec((r, d), lambda b, i: (0, 0))
    return pl.pallas_call(
        _conv_prompt_kernel,
        out_shape=jax.ShapeDtypeStruct((N_PROMPT, d), BF16),
        grid=(BATCH, tiles),
        in_specs=[
            pl.BlockSpec((CONV_HALO, d),
                         lambda b, i: (jnp.maximum((b * tiles + i) * halo_per_tile - 1, 0), 0)),
            pl.BlockSpec((t, d), lambda b, i: (b * tiles + i, 0)),
            vec(CONV_HALO), vec(CONV_W), vec(1), vec(1), vec(1)],
        out_specs=pl.BlockSpec((t, d), lambda b, i: (b * tiles + i, 0)),
        scratch_shapes=[pltpu.VMEM((CONV_HALO + t, d), F32)] + _conv_scratch(t, d),
        compiler_params=_params(2),
        name="conv_prompt",
    )(g, g, first, wdw, bdw, ln_g, ln_b)


def _conv_streams(win, wdw, bdw, ln_g, ln_b):
    s, rows, d = win.shape
    t = rows - CONV_HALO
    vec = lambda r: pl.BlockSpec((r, d), lambda i: (0, 0))
    return pl.pallas_call(
        _conv_stream_kernel,
        out_shape=jax.ShapeDtypeStruct((s * t, d), BF16),
        grid=(s,),
        in_specs=[pl.BlockSpec((None, rows, d), lambda i: (i, 0, 0)),
                  vec(CONV_W), vec(1), vec(1), vec(1)],
        out_specs=pl.BlockSpec((t, d), lambda i: (i, 0)),
        scratch_shapes=_conv_scratch(t, d),
        compiler_params=_params(1),
        name="conv_streams",
    )(win, wdw, bdw, ln_g, ln_b)


def _suffix_sum_matrix(n):
    r = lax.broadcasted_iota(jnp.int32, (2 * n, n), 0) % n
    c = lax.broadcasted_iota(jnp.int32, (2 * n, n), 1)
    return jnp.where(r > c, 1.0, 0.0).astype(BF16)


def _bits_op(x, op, bits):
    u = lax.bitcast_convert_type(x, jnp.uint32)
    return lax.bitcast_convert_type(op(u, jnp.uint32(bits)), F32)


def _scores(q, k):
    return lax.dot_general(q, k, (((1,), (1,)), ((), ())), preferred_element_type=F32)


def _sb_terms(z2, mask):
    neg_abs = _bits_op(z2, jnp.bitwise_or, SIGN_BIT)
    t2 = jnp.log(1.0 + jnp.exp2(neg_abs)) * LOG2E
    log_beta = jnp.minimum(z2, 0.0) - t2
    neg_log_keep = z2 - log_beta
    if mask is not None:
        neg_log_keep = jnp.where(mask, neg_log_keep, 0.0)
    hi = _bits_op(neg_log_keep, jnp.bitwise_and, BF16_BITS)
    lo = neg_log_keep - hi
    hilo = jnp.concatenate([hi.astype(BF16), lo.astype(BF16)], axis=1)
    return log_beta, neg_log_keep[:, 0:1], hilo


def _sb_log_weights(log_beta, first_col, after, carry, mask):
    x = log_beta - after - carry
    if mask is not None:
        x = jnp.where(mask, x, NEG_BIG)
    return x, carry + (after[:, 0:1] + first_col)


def _sb_first_half(scores_fn, upper2, carry, mask):
    log_beta, first_col, hilo = _sb_terms(scores_fn(), mask)
    after = jnp.dot(hilo, upper2, preferred_element_type=F32)
    return _sb_log_weights(log_beta, first_col, after, carry, mask)


def _sb_weighted_values(x, v):
    return jnp.dot(jnp.exp2(x).astype(BF16), v, preferred_element_type=F32)


def _strict_lower_mask(nq, nk, period):
    r = lax.broadcasted_iota(jnp.int32, (nq, nk), 0) % period
    c = lax.broadcasted_iota(jnp.int32, (nq, nk), 1)
    return c < r


def _first_keys_mask(nq, nk, n_valid):
    return lax.broadcasted_iota(jnp.int32, (nq, nk), 1) < n_valid


def _attn_prompt_kernel(q_ref, k_ref, v_ref, km_ref, vm_ref, u_ref, us_ref, o_ref,
                        acc_ref, carry_ref, x_ref):
    qi = pl.program_id(2)
    blk = ATT_BLOCK
    nh = ATT_PROMPT_HEADS
    cols = [slice(h * HEAD_DIM, (h + 1) * HEAD_DIM) for h in range(nh)]

    def block_rows(j):
        return pl.ds(pl.multiple_of(j * blk, blk), blk)

    def step(v_pending, pending_keys, k_next, next_rows, upper2_ref, mask):
        next_keys = upper2_ref.shape[1]
        scores, mid = {}, {}
        for t in range(nh + 2):
            if t < nh:
                scores[t] = _scores(q_ref[:, cols[t]], k_next[next_rows, cols[t]])
            h = t - 1
            if 0 <= h < nh:
                acc_ref[h] += _sb_weighted_values(x_ref[h, :, :pending_keys], v_pending(h))
                log_beta, first_col, hilo = _sb_terms(scores.pop(h), mask)
                mid[h] = (log_beta, first_col,
                          jnp.dot(hilo, upper2_ref[...], preferred_element_type=F32))
            h = t - 2
            if 0 <= h < nh:
                x, carry = _sb_log_weights(*mid.pop(h), carry_ref[h], mask)
                x_ref[h, :, :next_keys] = x
                carry_ref[h] = carry

    diag_mask = _strict_lower_mask(blk, blk, blk)
    for h in range(nh):
        acc_ref[h] = jnp.zeros((blk, HEAD_DIM), F32)
        x, carry = _sb_first_half(
            lambda: _scores(q_ref[:, cols[h]], k_ref[block_rows(qi), cols[h]]),
            u_ref[...], jnp.zeros((blk, 1), F32), diag_mask)
        x_ref[h] = x
        carry_ref[h] = carry

    def earlier(p):
        step(lambda h: v_ref[block_rows(p), cols[h]], blk, k_ref, block_rows(p - 1), u_ref, None)

    odd = qi % 2
    pl.when(odd == 1)(lambda: earlier(qi))
    first_pending = qi - odd

    def body(t, carry):
        earlier(first_pending - 2 * t)
        earlier(first_pending - 2 * t - 1)
        return carry

    lax.fori_loop(0, first_pending // 2, body, 0)

    step(lambda h: v_ref[block_rows(0), cols[h]], blk, km_ref, slice(None), us_ref,
         _first_keys_mask(blk, SMALL_BLOCK, N_META))
    for h in range(nh):
        out = _sb_weighted_values(x_ref[h, :, :SMALL_BLOCK], vm_ref[:, cols[h]])
        o_ref[:, cols[h]] = (acc_ref[h] + out).astype(o_ref.dtype)


def _attn_prompt(q, kv, upper2, upper2_small):
    blk = ATT_BLOCK
    tiles = SEQ // blk
    hw = ATT_PROMPT_HEADS * HEAD_DIM
    hb = N_HEADS // ATT_PROMPT_HEADS
    meta_blk = META_ROW0 // SMALL_BLOCK
    const = lambda a: pl.BlockSpec(a.shape, lambda b, h, i: (0, 0))
    return pl.pallas_call(
        _attn_prompt_kernel,
        out_shape=jax.ShapeDtypeStruct((N_PROMPT, D_MODEL), BF16),
        grid=(BATCH, hb, tiles),
        in_specs=[
            pl.BlockSpec((blk, hw), lambda b, h, i: (b * tiles + i, h)),
            pl.BlockSpec((SEQ, hw), lambda b, h, i: (b, h)),
            pl.BlockSpec((SEQ, hw), lambda b, h, i: (b, hb + h)),
            pl.BlockSpec((SMALL_BLOCK, hw), lambda b, h, i: (meta_blk, h)),
            pl.BlockSpec((SMALL_BLOCK, hw), lambda b, h, i: (meta_blk, hb + h)),
            const(upper2), const(upper2_small)],
        out_specs=pl.BlockSpec((blk, hw), lambda b, h, i: (b * tiles + i, h)),
        scratch_shapes=[pltpu.VMEM((ATT_PROMPT_HEADS, blk, HEAD_DIM), F32),
                        pltpu.VMEM((ATT_PROMPT_HEADS, blk, 1), F32),
                        pltpu.VMEM((ATT_PROMPT_HEADS, blk, blk), F32)],
        compiler_params=_params(3),
        name="attn_prompt",
    )(q, kv, kv, kv, kv, upper2, upper2_small)


def _head_rows(ref3, start, size, h):
    keys, nh, hd = ref3.shape
    ref2 = ref3.reshape(keys * nh, hd)
    return ref2[pl.ds(start * nh + h, size, stride=nh), :]


def _attn_sample_kernel(q_ref, kn_ref, vn_ref, kc_ref, vc_ref, km_ref, vm_ref, u_ref, us_ref,
                        o_ref):
    blk = ATT_BLOCK
    nh = ATT_SAMPLE_HEADS
    nq = nh * DEC_SEQ
    cols = [slice(h * HEAD_DIM, (h + 1) * HEAD_DIM) for h in range(nh)]
    qrows = [slice(h * DEC_SEQ, (h + 1) * DEC_SEQ) for h in range(nh)]
    fill = jnp.zeros((SMALL_BLOCK - DEC_SEQ, HEAD_DIM), BF16)

    def block(k_of, v_of, u, mask):
        scores = lambda: jnp.concatenate(
            [_scores(q_ref[:, cols[h]], k_of(h)) for h in range(nh)], axis=0)
        return scores, v_of, u, mask

    def cached(ref3, c):
        return lambda h: _head_rows(ref3, c * blk, blk, h).astype(BF16)

    blocks = [block(lambda h: jnp.concatenate([kn_ref[:, cols[h]], fill], axis=0),
                    lambda h: jnp.concatenate([vn_ref[:, cols[h]], fill], axis=0),
                    us_ref, _strict_lower_mask(nq, SMALL_BLOCK, DEC_SEQ))]
    for c in reversed(range(PAST_LEN // blk)):
        blocks.append(block(cached(kc_ref, c), cached(vc_ref, c), u_ref, None))
    blocks.append(block(lambda h: km_ref[:, cols[h]], lambda h: vm_ref[:, cols[h]],
                        us_ref, _first_keys_mask(nq, SMALL_BLOCK, N_META)))

    acc = jnp.zeros((nq, HEAD_DIM), F32)
    pending, carry = _sb_first_half(blocks[0][0], blocks[0][2][...], jnp.zeros((nq, 1), F32),
                                    blocks[0][3])
    for b in range(len(blocks)):
        nxt = None
        if b + 1 < len(blocks):
            nxt, carry = _sb_first_half(blocks[b + 1][0], blocks[b + 1][2][...], carry,
                                        blocks[b + 1][3])
        a = jnp.exp2(pending).astype(BF16)
        acc = acc + jnp.concatenate(
            [jnp.dot(a[qrows[h]], blocks[b][1](h), preferred_element_type=F32)
             for h in range(nh)], axis=0)
        pending = nxt
    for h in range(nh):
        o_ref[:, cols[h]] = acc[qrows[h]].astype(o_ref.dtype)


def _attn_sample(q, kv, cache_k, cache_v, upper2, upper2_small):
    nh = ATT_SAMPLE_HEADS
    hw = nh * HEAD_DIM
    hb = N_HEADS // nh
    row_blk0 = SAMPLE_ROW0 // DEC_SEQ
    meta_blk = META_ROW0 // SMALL_BLOCK
    new = lambda off: pl.BlockSpec((DEC_SEQ, hw), lambda s, h: (row_blk0 + s, off + h))
    meta = lambda off: pl.BlockSpec((SMALL_BLOCK, hw), lambda s, h: (meta_blk, off + h))
    cache = pl.BlockSpec((None, PAST_LEN, nh, HEAD_DIM), lambda s, h: (s, 0, h, 0))
    const = lambda a: pl.BlockSpec(a.shape, lambda s, h: (0, 0))
    return pl.pallas_call(
        _attn_sample_kernel,
        out_shape=jax.ShapeDtypeStruct((N_SAMPLE, D_MODEL), BF16),
        grid=(DEC_BATCH, hb),
        in_specs=[new(0), new(0), new(hb), cache, cache, meta(0), meta(hb),
                  const(upper2), const(upper2_small)],
        out_specs=pl.BlockSpec((DEC_SEQ, hw), lambda s, h: (s, h)),
        compiler_params=_params(2),
        name="attn_sample",
    )(q, kv, kv, cache_k, cache_v, kv, kv, upper2, upper2_small)


def kernel(x_prompt, x_sample, state_conv, cache_k, cache_v, meta, a_w_pw1, a_b_pw1, a_w_dw, a_b_dw, a_ln_g, a_ln_b, a_w_pw2, a_b_pw2, w_kv, w_q, w_o, ln_mix_g, ln_mix_b, ln_ffn_g, ln_ffn_b, w_gate, w_up, w_down):
    d = D_MODEL
    n_pad = N_ROWS - META_ROW0 - N_META
    zero_vec = jnp.zeros((1, d), F32)
    xp = x_prompt.reshape(N_PROMPT, d)
    x_tail = jnp.concatenate([x_sample.reshape(N_SAMPLE, d), meta, jnp.zeros((n_pad, d), F32)],
                             axis=0)
    tail_pad = jnp.zeros((N_TAIL - N_SAMPLE, d), BF16)

    def ffn(r_main, r_tail, xb, l):
        h = _swiglu(xb, w_gate, w_up, l)
        y = _matmul_stream(h, w_down, l, F32, DOWN_COL_TILE, "ffn_down")
        return r_main, r_tail, y, zero_vec, ln_ffn_g[l][None], ln_ffn_b[l][None]

    g = _conv_glu((xp.astype(BF16), x_tail.astype(BF16)), a_w_pw1[0], a_b_pw1[0][None])
    g_sample = g[SAMPLE_ROW0:META_ROW0].reshape(DEC_BATCH, DEC_SEQ, d)
    g_meta = g[META_ROW0:META_ROW0 + N_META]
    first = jnp.concatenate([jnp.zeros((CONV_HALO - N_META, d), F32), g_meta], axis=0)
    conv_args = (a_w_dw[0], a_b_dw[0][None], a_ln_g[0][None], a_ln_b[0][None])
    c_prompt = _conv_prompt(g, first, *conv_args)
    hist = jnp.concatenate([state_conv[0], jnp.zeros((1, CONV_W - 1, d), F32)], axis=0)
    new = jnp.concatenate([g_sample, g_meta[None]], axis=0)
    win = jnp.concatenate([jnp.zeros((DEC_BATCH + 1, CONV_LEAD, d), F32), hist, new], axis=1)
    c_tail = jnp.concatenate([_conv_streams(win, *conv_args), jnp.zeros((n_pad, d), BF16)],
                             axis=0)
    (m,) = _matmul((c_prompt, c_tail), a_w_pw2[0].astype(BF16), [F32], WIDE_COL_TILE,
                   "conv_pw2")
    x1, x1b = _add_ln(xp, x_tail, m, a_b_pw2[0][None], ln_mix_g[0][None], ln_mix_b[0][None],
                      [F32, BF16], "mix_ln")
    x2, x2b = _add_ln(*ffn(x1, x1[N_PROMPT:], x1b, 0), [F32, BF16], "ffn_ln")

    kvb, k_s, v_s, k_p, v_p = _kv_proj(x2b, w_kv)
    qb = _matmul_stream(x2b, w_q, 0, BF16, WIDE_COL_TILE, "q_proj", scale=SCORE_SCALE)
    upper2 = _suffix_sum_matrix(ATT_BLOCK)
    upper2_small = _suffix_sum_matrix(SMALL_BLOCK)
    o_prompt = _attn_prompt(qb, kvb, upper2, upper2_small)
    o_sample = _attn_sample(qb, kvb, cache_k, cache_v, upper2, upper2_small)
    o_tail = jnp.concatenate([o_sample, tail_pad], axis=0)
    (m,) = _matmul((o_prompt, o_tail), w_o[0].astype(BF16), [F32], WIDE_COL_TILE, "o_proj")
    x3, x3b = _add_ln(x2, x2[N_PROMPT:], m, zero_vec, ln_mix_g[1][None], ln_mix_b[1][None],
                      [F32, BF16], "mix_ln")
    _, _, y, bias, ln_g, ln_b = ffn(x3, None, x3b, 1)
    y_prompt, y_sample = _final_ln(x3, y, bias, ln_g, ln_b)

    y_prompt = y_prompt.reshape(BATCH, SEQ, d)
    y_sample = y_sample.reshape(DEC_BATCH, DEC_SEQ, d)
    n_ctx = CONV_W - 1
    state_conv_prompt = jnp.stack(
        [g[(b + 1) * SEQ - n_ctx:(b + 1) * SEQ] for b in range(BATCH)], axis=0)[None]
    state_conv_sample = jnp.concatenate([state_conv[0][:, DEC_SEQ:], g_sample], axis=1)[None]

    sample_shape = (DEC_BATCH, DEC_SEQ, N_HEADS, HEAD_DIM)
    return (y_prompt, y_sample, state_conv_prompt, state_conv_sample,
            k_p, v_p, k_s.reshape(sample_shape), v_s.reshape(sample_shape))
```
